```python
import math
import jax
import jax.numpy as jnp
from jax import lax
import numpy as np

D_MODEL = 1024
BATCH = 4
SEQ = 4096
DEPTH = 1
DEC_BATCH = 2
DEC_SEQ = 16384
PAST_LEN = 128

HEAD_DIM = 64
A_HEADS = D_MODEL // 128
A_KV_HEADS = 2
A_GROUP = A_HEADS // A_KV_HEADS
A_WIDTH = A_HEADS * HEAD_DIM
A_KV_WIDTH = A_KV_HEADS * HEAD_DIM
WINDOW = 128
BLOCK = 128
B_HEADS = D_MODEL // 128
Q_LORA = 3 * D_MODEL // 8
KV_LORA = D_MODEL // 4
NOPE_DIM = 64
ROPE_DIM = 32
V_DIM = 64
B_WIDTH = B_HEADS * V_DIM
ROPE_THETA = 10000.0
EPS = 1e-6

IN_WIDTHS = (A_WIDTH, A_KV_WIDTH, A_KV_WIDTH, A_WIDTH, Q_LORA, KV_LORA, ROPE_DIM, B_WIDTH, 2 * D_MODEL)
IN_SPLITS = tuple(sum(IN_WIDTHS[: i + 1]) for i in range(len(IN_WIDTHS) - 1))
D_IN = sum(IN_WIDTHS)

kernel_name = "hybrid_swa_mla_gated_encoder"


def rms_norm(x, gain):
    x32 = x.astype(jnp.float32)
    y = x32 * lax.rsqrt(jnp.mean(x32 * x32, axis=-1, keepdims=True) + EPS)
    return (y * gain.astype(jnp.float32)).astype(x.dtype)


def alibi_slopes(n):
    start = 2.0 ** (-8.0 / n)
    return start ** jnp.arange(1, n + 1, dtype=jnp.float32)


def rope_tables(S, dtype):
    inv = 1.0 / (ROPE_THETA ** (jnp.arange(0, ROPE_DIM, 2, dtype=jnp.float32) / ROPE_DIM))
    ang = jnp.arange(S, dtype=jnp.float32)[:, None] * inv[None, :]
    return jnp.cos(ang).astype(dtype), jnp.sin(ang).astype(dtype)


def apply_rope(x, cos, sin):
    x1, x2 = jnp.split(x, 2, axis=-1)
    return jnp.concatenate([x1 * cos - x2 * sin, x1 * sin + x2 * cos], axis=-1)


def windowed_gqa(q, k, v, sink):
    B, S, _ = q.shape
    nb = S // BLOCK
    q = q.reshape(B, nb, BLOCK, A_KV_HEADS, A_GROUP, HEAD_DIM) * (HEAD_DIM ** -0.5)
    pad = ((0, 0), (BLOCK, BLOCK), (0, 0), (0, 0))
    kp = jnp.pad(k.reshape(B, S, A_KV_HEADS, HEAD_DIM), pad).reshape(B, nb + 2, BLOCK, A_KV_HEADS, HEAD_DIM)
    vp = jnp.pad(v.reshape(B, S, A_KV_HEADS, HEAD_DIM), pad).reshape(B, nb + 2, BLOCK, A_KV_HEADS, HEAD_DIM)
    kb = jnp.concatenate([kp[:, :-2], kp[:, 1:-1], kp[:, 2:]], axis=2)
    vb = jnp.concatenate([vp[:, :-2], vp[:, 1:-1], vp[:, 2:]], axis=2)
    s = jnp.einsum("bnqhgd,bnkhd->bnhgqk", q, kb).astype(jnp.float32)
    qi = jnp.arange(BLOCK)
    kj = jnp.arange(3 * BLOCK)
    rel = kj[None, :] - BLOCK - qi[:, None]
    kpos = jnp.arange(nb)[:, None] * BLOCK - BLOCK + kj[None, :]
    valid = (jnp.abs(rel) <= WINDOW)[None] & ((kpos >= 0) & (kpos < S))[:, None, :]
    dist = jnp.abs(rel).astype(jnp.float32)
    bias = (-alibi_slopes(A_HEADS)[:, None, None] * dist[None]).reshape(A_KV_HEADS, A_GROUP, BLOCK, 3 * BLOCK)
    s = jnp.where(valid[None, :, None, None], s + bias[None, None], -1e30)
    sink_l = sink.astype(jnp.float32).reshape(A_KV_HEADS, A_GROUP)[None, None, :, :, None, None]
    m = jnp.maximum(jnp.max(s, axis=-1, keepdims=True), sink_l)
    p = jnp.exp(s - m)
    denom = jnp.sum(p, axis=-1, keepdims=True) + jnp.exp(sink_l - m)
    p = (p / denom).astype(vb.dtype)
    o = jnp.einsum("bnhgqk,bnkhd->bnqhgd", p, vb)
    return o.reshape(B, S, A_WIDTH)


def mla(cq, ckv, kr, g_q, w_uq, g_kv, w_ukv):
    B, S, _ = cq.shape
    nb = S // BLOCK
    q = (rms_norm(cq, g_q) @ w_uq).reshape(B, S, B_HEADS, NOPE_DIM + ROPE_DIM)
    q_nope, q_rope = q[..., :NOPE_DIM], q[..., NOPE_DIM:]
    kv = (rms_norm(ckv, g_kv) @ w_ukv).reshape(B, S, B_HEADS, NOPE_DIM + V_DIM)
    k_nope, v = kv[..., :NOPE_DIM], kv[..., NOPE_DIM:]
    cos, sin = rope_tables(S, cq.dtype)
    q_rope = apply_rope(q_rope, cos[:, None, :], sin[:, None, :])
    k_rope = apply_rope(kr, cos, sin)
    scale = (NOPE_DIM + ROPE_DIM) ** -0.5
    q = jnp.concatenate([q_nope, q_rope], axis=-1) * scale
    k = jnp.concatenate([k_nope, jnp.broadcast_to(k_rope[:, :, None, :], (B, S, B_HEADS, ROPE_DIM))], axis=-1)
    qb = q.reshape(B, nb, BLOCK, B_HEADS, NOPE_DIM + ROPE_DIM).transpose(1, 0, 2, 3, 4)

    def attend(q_blk):
        s = jnp.einsum("bqhd,bkhd->bhqk", q_blk, k).astype(jnp.float32)
        p = jax.nn.softmax(s, axis=-1).astype(v.dtype)
        return jnp.einsum("bhqk,bkhd->bqhd", p, v)

    o = lax.map(attend, qb)
    return o.transpose(1, 0, 2, 3, 4).reshape(B, S, B_WIDTH)


def encoder_layer(x, c, w_ada, b_ada, g_norm, w_in, g_q, w_uq, g_kv, w_ukv, sink, w_oa, w_ob, w_out):
    mod = jax.nn.silu(c) @ w_ada + b_ada
    shift, scale, gate_res = jnp.split(mod, 3, axis=-1)
    h = rms_norm(x, g_norm) * (1.0 + scale[:, None, :]) + shift[:, None, :]
    proj = h @ w_in
    qa, ka, va, za, cq, ckv, kr, zb, gm = jnp.split(proj, IN_SPLITS, axis=-1)
    ya = windowed_gqa(qa, ka, va, sink) * jax.nn.silu(za)
    yb = mla(cq, ckv, kr, g_q, w_uq, g_kv, w_ukv) * jax.nn.silu(zb)
    ga, gb = jnp.split(jax.nn.sigmoid(gm), 2, axis=-1)
    merged = ga * (ya @ w_oa) + gb * (yb @ w_ob)
    return x + gate_res[:, None, :] * (merged @ w_out)


def setup_inputs(seed: int = 0) -> dict:
    key = jax.random.key(seed)
    ks = jax.random.split(key, 20)
    f32 = jnp.float32
    nrm = lambda k, shape, s: jax.random.normal(k, shape, f32) * s
    return {
        "x_prompt": nrm(ks[0], (BATCH, SEQ, D_MODEL), 1.0),
        "x_sample": nrm(ks[1], (DEC_BATCH, DEC_SEQ, D_MODEL), 1.0),
        "c_prompt": nrm(ks[2], (BATCH, D_MODEL), 1.0),
        "c_sample": nrm(ks[3], (DEC_BATCH, D_MODEL), 1.0),
        "w_ada": nrm(ks[4], (DEPTH, D_MODEL, 3 * D_MODEL), 0.5 * D_MODEL ** -0.5),
        "b_ada": nrm(ks[5], (DEPTH, 3 * D_MODEL), 0.01),
        "g_norm": 1.0 + nrm(ks[6], (DEPTH, D_MODEL), 0.01),
        "w_in": nrm(ks[7], (DEPTH, D_MODEL, D_IN), D_MODEL ** -0.5),
        "g_q": 1.0 + nrm(ks[8], (DEPTH, Q_LORA), 0.01),
        "w_uq": nrm(ks[9], (DEPTH, Q_LORA, B_HEADS * (NOPE_DIM + ROPE_DIM)), Q_LORA ** -0.5),
        "g_kv": 1.0 + nrm(ks[10], (DEPTH, KV_LORA), 0.01),
        "w_ukv": nrm(ks[11], (DEPTH, KV_LORA, B_HEADS * (NOPE_DIM + V_DIM)), KV_LORA ** -0.5),
        "sink": nrm(ks[12], (DEPTH, A_HEADS), 0.5),
        "w_oa": nrm(ks[13], (DEPTH, A_WIDTH, D_MODEL), A_WIDTH ** -0.5),
        "w_ob": nrm(ks[14], (DEPTH, B_WIDTH, D_MODEL), B_WIDTH ** -0.5),
        "w_out": nrm(ks[15], (DEPTH, D_MODEL, D_MODEL), D_MODEL ** -0.5),
        "g_final": 1.0 + nrm(ks[16], (D_MODEL,), 0.01),
    }


def reference(x_prompt, x_sample, c_prompt, c_sample, w_ada, b_ada, g_norm, w_in, g_q, w_uq, g_kv, w_ukv,
              sink, w_oa, w_ob, w_out, g_final):
    def trunk(x, c):
        for l in range(DEPTH):
            x = encoder_layer(x, c, w_ada[l], b_ada[l], g_norm[l], w_in[l], g_q[l], w_uq[l], g_kv[l], w_ukv[l],
                              sink[l], w_oa[l], w_ob[l], w_out[l])
        return rms_norm(x, g_final)

    y_prompt = trunk(x_prompt, c_prompt)
    y_sample = trunk(x_sample, c_sample)
    return (y_prompt, y_sample)
```

```python
import functools
import math

import jax
import jax.numpy as jnp
from jax import lax
from jax.experimental import pallas as pl
from jax.experimental.pallas import tpu as pltpu

D_MODEL = 1024
HEAD_DIM = 64
A_HEADS = 8
A_KV_HEADS = 2
A_GROUP = A_HEADS // A_KV_HEADS
A_WIDTH = A_HEADS * HEAD_DIM
A_KV_WIDTH = A_KV_HEADS * HEAD_DIM
WINDOW = 128
BLOCK = 128
B_HEADS = 8
Q_LORA = 384
KV_LORA = 256
NOPE_DIM = 64
ROPE_DIM = 32
V_DIM = 64
B_WIDTH = B_HEADS * V_DIM
ROPE_THETA = 10000.0
EPS = 1e-6
NEG_BIG = -1e30

LANES = 128
VMEM_LIMIT_BYTES = 48 * 1024 * 1024

C_QA = 0
C_KA = C_QA + A_WIDTH
C_VA = C_KA + 4 * LANES
C_ZA = C_VA + 4 * LANES
C_CQ = C_ZA + A_WIDTH
C_CKV = C_CQ + Q_LORA
C_KR = C_CKV + KV_LORA
C_KRS = C_KR + LANES
C_ZB = C_KRS + LANES
C_GM = C_ZB + B_WIDTH
C_END = C_GM + 2 * D_MODEL

BF16 = jnp.bfloat16
F32 = jnp.float32


def _silu(x):
    return x * (1.0 / (1.0 + jnp.exp(-x)))


def _sigmoid(x):
    return 1.0 / (1.0 + jnp.exp(-x))


def _dot(a, b):
    return jnp.dot(a, b, preferred_element_type=F32)


def _dot_nt(a, b):
    return lax.dot_general(a, b, (((1,), (1,)), ((), ())), preferred_element_type=F32)


def _mod_kernel(c_ref, w_ref, b_ref, o_ref):
    c = c_ref[...]
    o_ref[...] = jnp.dot(_silu(c), w_ref[...], preferred_element_type=F32,
                         precision=lax.Precision.HIGHEST) + b_ref[...]


def _mod_call(c_pad, w_ada, b_ada):
    rows = c_pad.shape[0]
    n = w_ada.shape[1]
    bn = 512
    return pl.pallas_call(
        _mod_kernel,
        out_shape=jax.ShapeDtypeStruct((rows, n), F32),
        grid=(n // bn,),
        in_specs=[
            pl.BlockSpec((rows, D_MODEL), lambda j: (0, 0)),
            pl.BlockSpec((D_MODEL, bn), lambda j: (0, j)),
            pl.BlockSpec((1, bn), lambda j: (0, j)),
        ],
        out_specs=pl.BlockSpec((rows, bn), lambda j: (0, j)),
        compiler_params=pltpu.CompilerParams(dimension_semantics=("arbitrary",),
                                             vmem_limit_bytes=VMEM_LIMIT_BYTES),
        name="mod",
    )(c_pad, w_ada, b_ada)


def _rms(x, gain):
    return x * lax.rsqrt(jnp.mean(x * x, axis=-1, keepdims=True) + EPS) * gain


def _inproj_kernel(x_ref, mod_ref, gn_ref, win_ref, gq_ref, wq_ref, wqs_ref, gkv_ref, wk_ref, wv_ref,
                   qcos_ref, qsin_ref, kcos_ref, ksin_ref,
                   qa_ref, ka_ref, va_ref, za_ref, qm_ref, km_ref, vm_ref, zb_ref, gm_ref):
    x = x_ref[0]
    mod = mod_ref[0]
    shift = mod[:, :D_MODEL]
    scale = mod[:, D_MODEL:2 * D_MODEL]
    h = (_rms(x, gn_ref[...]) * (1.0 + scale) + shift).astype(BF16)

    def seg(lo, width):
        return _dot(h, win_ref[:, lo:lo + width])

    qa_ref[0] = seg(C_QA, A_WIDTH).astype(BF16)
    ka_ref[0] = seg(C_KA, 4 * LANES).astype(BF16)
    va_ref[0] = seg(C_VA, 4 * LANES).astype(BF16)
    za_ref[0] = _silu(seg(C_ZA, A_WIDTH))
    zb_ref[0] = _silu(seg(C_ZB, B_WIDTH))
    gm_ref[0] = _sigmoid(seg(C_GM, 2 * D_MODEL))

    cqn = _rms(seg(C_CQ, Q_LORA), gq_ref[...]).astype(BF16)
    qcos = qcos_ref[...]
    qsin = qsin_ref[...]
    for hd in range(B_HEADS):
        sl = slice(hd * LANES, (hd + 1) * LANES)
        qh = _dot(cqn, wq_ref[:, sl]) * qcos + _dot(cqn, wqs_ref[:, sl]) * qsin
        qm_ref[0, :, sl] = qh.astype(BF16)

    ckvn = _rms(seg(C_CKV, KV_LORA), gkv_ref[...]).astype(BF16)
    kr = seg(C_KR, LANES) * kcos_ref[...] + seg(C_KRS, LANES) * ksin_ref[...]
    for hd in range(B_HEADS):
        sl = slice(hd * LANES, (hd + 1) * LANES)
        km_ref[0, :, sl] = (_dot(ckvn, wk_ref[:, sl]) + kr).astype(BF16)
    vm_ref[0] = _dot(ckvn, wv_ref[...]).astype(BF16)


def _inproj_call(x, mod3, g_norm, w_in_x, g_q, wq, wqs, g_kv, wk, wv, tabs, tm):
    B, S, _ = x.shape
    nt = S // tm
    const2 = lambda b, i: (0, 0)
    tok = lambda w: pl.BlockSpec((1, tm, w), lambda b, i: (b, i, 0))
    tab = pl.BlockSpec((tm, LANES), lambda b, i: (i, 0))
    full = lambda a: pl.BlockSpec(a.shape, const2)
    out_widths = [(A_WIDTH, BF16), (4 * LANES, BF16), (4 * LANES, BF16), (A_WIDTH, F32),
                  (B_HEADS * LANES, BF16), (B_HEADS * LANES, BF16), (B_WIDTH, BF16), (B_WIDTH, F32),
                  (2 * D_MODEL, F32)]
    return pl.pallas_call(
        _inproj_kernel,
        out_shape=[jax.ShapeDtypeStruct((B, S, w), dt) for w, dt in out_widths],
        grid=(B, nt),
        in_specs=[tok(D_MODEL),
                  pl.BlockSpec((1, 1, 3 * D_MODEL), lambda b, i: (b, 0, 0)),
                  full(g_norm), full(w_in_x), full(g_q), full(wq), full(wqs), full(g_kv), full(wk), full(wv),
                  tab, tab, tab, tab],
        out_specs=[tok(w) for w, _ in out_widths],
        compiler_params=pltpu.CompilerParams(dimension_semantics=("arbitrary", "arbitrary"),
                                             vmem_limit_bytes=VMEM_LIMIT_BYTES),
        name="inproj",
    )(x, mod3, g_norm, w_in_x, g_q, wq, wqs, g_kv, wk, wv, *tabs)


def _swa_kernel(sink_ref, q_ref, kp_ref, kc_ref, kn_ref, vp_ref, vc_ref, vn_ref, za_ref, o_ref, *, nsub, slopes):
    i = pl.program_id(1)
    last = pl.num_programs(1) - 1
    row = lax.broadcasted_iota(jnp.int32, (BLOCK, BLOCK), 0)
    col = lax.broadcasted_iota(jnp.int32, (BLOCK, BLOCK), 1)
    d = col - row
    dist = [jnp.abs(d - BLOCK).astype(F32), jnp.abs(d).astype(F32), jnp.abs(d + BLOCK).astype(F32)]
    far = 4 * BLOCK
    lo_edge = jnp.where(i > 0, 0, far)
    hi_edge = jnp.where(i < last, 0, -far)

    def kv_block(p_ref, c_ref, n_ref, jj, lanes):
        if jj < 0:
            return p_ref[0, :, lanes]
        if jj >= nsub:
            return n_ref[0, :, lanes]
        return c_ref[0, jj * BLOCK:(jj + 1) * BLOCK, lanes]

    for j in range(nsub):
        rows = slice(j * BLOCK, (j + 1) * BLOCK)
        valid = [d >= (lo_edge if j == 0 else 0), None, d <= (hi_edge if j == nsub - 1 else 0)]
        for pair in range(A_HEADS // 2):
            kvh = (2 * pair) // A_GROUP
            qp = q_ref[0, rows, pair * LANES:(pair + 1) * LANES]
            acc = None
            for e in range(2):
                head = 2 * pair + e
                lanes = slice((2 * kvh + e) * LANES, (2 * kvh + e + 1) * LANES)
                sink = sink_ref[head]
                s_blocks = []
                for t in range(3):
                    kb = kv_block(kp_ref, kc_ref, kn_ref, j - 1 + t, lanes)
                    s = _dot_nt(qp, kb) - slopes[head] * dist[t]
                    if valid[t] is not None:
                        s = jnp.where(valid[t], s, NEG_BIG)
                    s_blocks.append(s)
                m = jnp.maximum(jnp.maximum(s_blocks[0], s_blocks[1]), s_blocks[2])
                m = jnp.maximum(jnp.max(m, axis=-1, keepdims=True), sink)
                denom = jnp.exp(sink - m)
                o = None
                for t in range(3):
                    p = jnp.exp(s_blocks[t] - m)
                    denom = denom + jnp.sum(p, axis=-1, keepdims=True)
                    vb = kv_block(vp_ref, vc_ref, vn_ref, j - 1 + t, lanes)
                    pv = _dot(p.astype(BF16), vb)
                    o = pv if o is None else o + pv
                o = o * (1.0 / denom)
                acc = o if acc is None else acc + o
            gate = za_ref[0, rows, pair * LANES:(pair + 1) * LANES]
            o_ref[0, rows, pair * LANES:(pair + 1) * LANES] = (acc * gate).astype(BF16)


def _swa_call(sink, qa, ka, va, za, tq):
    B, S, _ = qa.shape
    nsub = tq // BLOCK
    nt = S // tq
    nblk = S // BLOCK
    slopes = tuple(float((2.0 ** (-8.0 / A_HEADS)) ** (k + 1)) for k in range(A_HEADS))
    cur = lambda w: pl.BlockSpec((1, tq, w), lambda b, i: (b, i, 0))
    prev = pl.BlockSpec((1, BLOCK, 4 * LANES), lambda b, i: (b, jnp.maximum(i * nsub - 1, 0), 0))
    nxt = pl.BlockSpec((1, BLOCK, 4 * LANES), lambda b, i: (b, jnp.minimum((i + 1) * nsub, nblk - 1), 0))
    return pl.pallas_call(
        functools.partial(_swa_kernel, nsub=nsub, slopes=slopes),
        out_shape=jax.ShapeDtypeStruct((B, S, A_WIDTH), BF16),
        grid=(B, nt),
        in_specs=[pl.BlockSpec(memory_space=pltpu.SMEM),
                  cur(A_WIDTH), prev, cur(4 * LANES), nxt, prev, cur(4 * LANES), nxt, cur(A_WIDTH)],
        out_specs=cur(A_WIDTH),
        compiler_params=pltpu.CompilerParams(dimension_semantics=("arbitrary", "arbitrary"),
                                             vmem_limit_bytes=VMEM_LIMIT_BYTES),
        name="swa",
    )(sink, qa, ka, ka, ka, va, va, va, za)


def _mla_kernel(q_ref, k_ref, v_ref, zb_ref, o_ref, m_ref, l_ref, acc_ref, *, bk, nk):
    bq = q_ref.shape[1]
    m_ref[...] = jnp.full(m_ref.shape, NEG_BIG, F32)
    l_ref[...] = jnp.zeros(l_ref.shape, F32)
    acc_ref[...] = jnp.zeros(acc_ref.shape, F32)

    def body(c, carry):
        start = pl.multiple_of(c * bk, bk)
        v = v_ref[0, pl.ds(start, bk), :]
        for e in range(2):
            lanes = slice(e * LANES, (e + 1) * LANES)
            q = q_ref[0, :, lanes]
            k = k_ref[0, pl.ds(start, bk), lanes]
            s = _dot_nt(q, k)
            m_old = m_ref[e]
            m_new = jnp.maximum(m_old, jnp.max(s, axis=-1, keepdims=True))
            alpha = jnp.exp2(m_old - m_new)
            p = jnp.exp2(s - m_new)
            l_ref[e] = alpha * l_ref[e] + jnp.sum(p, axis=-1, keepdims=True)
            acc_ref[e] = alpha * acc_ref[e] + _dot(p.astype(BF16), v)
            m_ref[e] = m_new
        return carry

    lax.fori_loop(0, nk, body, 0)

    o0 = acc_ref[0] * (1.0 / l_ref[0])
    o1 = acc_ref[1] * (1.0 / l_ref[1])
    lane = lax.broadcasted_iota(jnp.int32, (bq, LANES), 1)
    o = jnp.where(lane < V_DIM, o0, o1)
    o_ref[0] = (o * zb_ref[0]).astype(BF16)


def _mla_call(qm, km, vm, zb, bq, bk):
    B, S, _ = qm.shape
    npair = B_HEADS // 2
    nq = S // bq
    nk = S // bk
    return pl.pallas_call(
        functools.partial(_mla_kernel, bk=bk, nk=nk),
        out_shape=jax.ShapeDtypeStruct((B, S, B_WIDTH), BF16),
        grid=(B, npair, nq),
        in_specs=[pl.BlockSpec((1, bq, 2 * LANES), lambda b, p, i: (b, i, p)),
                  pl.BlockSpec((1, S, 2 * LANES), lambda b, p, i: (b, 0, p)),
                  pl.BlockSpec((1, S, LANES), lambda b, p, i: (b, 0, p)),
                  pl.BlockSpec((1, bq, LANES), lambda b, p, i: (b, i, p))],
        out_specs=pl.BlockSpec((1, bq, LANES), lambda b, p, i: (b, i, p)),
        scratch_shapes=[pltpu.VMEM((2, bq, 1), F32), pltpu.VMEM((2, bq, 1), F32),
                        pltpu.VMEM((2, bq, LANES), F32)],
        compiler_params=pltpu.CompilerParams(dimension_semantics=("arbitrary", "arbitrary", "arbitrary"),
                                             vmem_limit_bytes=VMEM_LIMIT_BYTES),
        name="mla",
    )(qm, km, vm, zb)


def _outproj_kernel(x_ref, mod_ref, ya_ref, yb_ref, gm_ref, woa_ref, wob_ref, wout_ref, gf_ref, o_ref, *, final):
    ga = gm_ref[0, :, :D_MODEL]
    gb = gm_ref[0, :, D_MODEL:]
    merged = ga * _dot(ya_ref[0], woa_ref[...]) + gb * _dot(yb_ref[0], wob_ref[...])
    gate_res = mod_ref[0][:, 2 * D_MODEL:]
    y = x_ref[0] + gate_res * _dot(merged.astype(BF16), wout_ref[...])
    if final:
        y = _rms(y, gf_ref[...])
    o_ref[0] = y


def _outproj_call(x, mod3, ya, yb, gm, w_oa, w_ob, w_out, g_final, tm, final):
    B, S, _ = x.shape
    nt = S // tm
    const2 = lambda b, i: (0, 0)
    tok = lambda w: pl.BlockSpec((1, tm, w), lambda b, i: (b, i, 0))
    full = lambda a: pl.BlockSpec(a.shape, const2)
    return pl.pallas_call(
        functools.partial(_outproj_kernel, final=final),
        out_shape=jax.ShapeDtypeStruct((B, S, D_MODEL), F32),
        grid=(B, nt),
        in_specs=[tok(D_MODEL),
                  pl.BlockSpec((1, 1, 3 * D_MODEL), lambda b, i: (b, 0, 0)),
                  tok(A_WIDTH), tok(B_WIDTH), tok(2 * D_MODEL),
                  full(w_oa), full(w_ob), full(w_out), full(g_final)],
        out_specs=tok(D_MODEL),
        compiler_params=pltpu.CompilerParams(dimension_semantics=("arbitrary", "arbitrary"),
                                             vmem_limit_bytes=VMEM_LIMIT_BYTES),
        name="outproj",
    )(x, mod3, ya, yb, gm, w_oa, w_ob, w_out, g_final)


def _pair_ext(w):
    z = jnp.zeros((w.shape[0], HEAD_DIM), w.dtype)
    groups = []
    for kvh in range(A_KV_HEADS):
        wk = w[:, kvh * HEAD_DIM:(kvh + 1) * HEAD_DIM]
        groups += [wk, z, z, wk]
    return jnp.concatenate(groups, axis=1)


def _swap_halves(w):
    half = w.shape[-1] // 2
    return jnp.concatenate([w[..., half:], w[..., :half]], axis=-1)


def _place_rope(w):
    d = w.shape[0]
    return jnp.concatenate([jnp.zeros((d, NOPE_DIM), w.dtype), w, jnp.zeros((d, LANES - NOPE_DIM - ROPE_DIM), w.dtype)], axis=1)


def _prep_layer(w_in, w_uq, w_ukv):
    o = 0
    parts = {}
    for name, width in (("qa", A_WIDTH), ("ka", A_KV_WIDTH), ("va", A_KV_WIDTH), ("za", A_WIDTH), ("cq", Q_LORA),
                        ("ckv", KV_LORA), ("kr", ROPE_DIM), ("zb", B_WIDTH), ("gm", 2 * D_MODEL)):
        parts[name] = w_in[:, o:o + width]
        o += width
    w_in_x = jnp.concatenate([
        parts["qa"] * (HEAD_DIM ** -0.5), _pair_ext(parts["ka"]), _pair_ext(parts["va"]), parts["za"],
        parts["cq"], parts["ckv"], _place_rope(parts["kr"]), _place_rope(_swap_halves(parts["kr"])),
        parts["zb"], parts["gm"]], axis=1).astype(BF16)
    assert w_in_x.shape[1] == C_END

    uq = w_uq.reshape(Q_LORA, B_HEADS, NOPE_DIM + ROPE_DIM)
    zq = jnp.zeros((Q_LORA, B_HEADS, LANES - NOPE_DIM - ROPE_DIM), w_uq.dtype)
    wq = jnp.concatenate([uq, zq], axis=-1).reshape(Q_LORA, B_HEADS * LANES).astype(BF16)
    wqs = jnp.concatenate([jnp.zeros((Q_LORA, B_HEADS, NOPE_DIM), w_uq.dtype), _swap_halves(uq[..., NOPE_DIM:]), zq],
                          axis=-1).reshape(Q_LORA, B_HEADS * LANES).astype(BF16)
    ukv = w_ukv.reshape(KV_LORA, B_HEADS, NOPE_DIM + V_DIM)
    wk = jnp.concatenate([ukv[..., :NOPE_DIM], jnp.zeros((KV_LORA, B_HEADS, LANES - NOPE_DIM), w_ukv.dtype)],
                         axis=-1).reshape(KV_LORA, B_HEADS * LANES).astype(BF16)
    wv = ukv[..., NOPE_DIM:].reshape(KV_LORA, B_WIDTH).astype(BF16)
    return w_in_x, wq, wqs, wk, wv


def _rope_tables(S):
    inv = 1.0 / (ROPE_THETA ** (jnp.arange(0, ROPE_DIM, 2, dtype=F32) / ROPE_DIM))
    ang = jnp.arange(S, dtype=F32)[:, None] * inv[None, :]
    cos, sin = jnp.cos(ang), jnp.sin(ang)
    ones = jnp.ones((S, NOPE_DIM), F32)
    zeros_n = jnp.zeros((S, NOPE_DIM), F32)
    zeros_p = jnp.zeros((S, LANES - NOPE_DIM - ROPE_DIM), F32)
    cos_t = jnp.concatenate([ones, cos, cos, zeros_p], axis=1)
    sin_t = jnp.concatenate([zeros_n, -sin, sin, zeros_p], axis=1)
    qscale = (NOPE_DIM + ROPE_DIM) ** -0.5 * math.log2(math.e)
    return cos_t * qscale, sin_t * qscale, cos_t, sin_t


def _trunk(x, c, layers, g_final, mod_all, row0):
    B, S, _ = x.shape
    tabs = _rope_tables(S)
    tm = 256
    n_layers = len(layers)
    for l, lw in enumerate(layers):
        mod3 = mod_all[l][row0:row0 + B].reshape(B, 1, 3 * D_MODEL)
        qa, ka, va, za, qm, km, vm, zb, gm = _inproj_call(
            x, mod3, lw["g_norm"], lw["w_in_x"], lw["g_q"], lw["wq"], lw["wqs"], lw["g_kv"], lw["wk"], lw["wv"],
            tabs, tm)
        ya = _swa_call(lw["sink"], qa, ka, va, za, 512)
        yb = _mla_call(qm, km, vm, zb, 512, 512)
        x = _outproj_call(x, mod3, ya, yb, gm, lw["w_oa"], lw["w_ob"], lw["w_out"], g_final, tm,
                          final=(l == n_layers - 1))
    return x


def kernel(x_prompt, x_sample, c_prompt, c_sample, w_ada, b_ada, g_norm, w_in, g_q, w_uq, g_kv, w_ukv, sink,
           w_oa, w_ob, w_out, g_final):
    depth = w_in.shape[0]
    nb_p, nb_s = c_prompt.shape[0], c_sample.shape[0]
    rows = -(-(nb_p + nb_s) // 8) * 8
    c_pad = jnp.concatenate([c_prompt, c_sample, jnp.zeros((rows - nb_p - nb_s, D_MODEL), F32)], axis=0)
    layers, mod_all = [], []
    for l in range(depth):
        w_in_x, wq, wqs, wk, wv = _prep_layer(w_in[l], w_uq[l], w_ukv[l])
        layers.append(dict(
            g_norm=g_norm[l].reshape(1, D_MODEL), w_in_x=w_in_x, g_q=g_q[l].reshape(1, Q_LORA), wq=wq, wqs=wqs,
            g_kv=g_kv[l].reshape(1, KV_LORA), wk=wk, wv=wv, sink=sink[l],
            w_oa=w_oa[l].astype(BF16), w_ob=w_ob[l].astype(BF16), w_out=w_out[l].astype(BF16)))
        mod_all.append(_mod_call(c_pad, w_ada[l], b_ada[l].reshape(1, 3 * D_MODEL)))
    gf = g_final.reshape(1, D_MODEL)
    y_prompt = _trunk(x_prompt, c_prompt, layers, gf, mod_all, 0)
    y_sample = _trunk(x_sample, c_sample, layers, gf, mod_all, nb_p)
    return (y_prompt, y_sample)
```

```python
import functools
import math

import jax
import jax.numpy as jnp
from jax import lax
from jax.experimental import pallas as pl
from jax.experimental.pallas import tpu as pltpu

D_MODEL = 1024
HEAD_DIM = 64
A_HEADS = 8
A_KV_HEADS = 2
A_GROUP = A_HEADS // A_KV_HEADS
A_WIDTH = A_HEADS * HEAD_DIM
A_KV_WIDTH = A_KV_HEADS * HEAD_DIM
WINDOW = 128
BLOCK = 128
B_HEADS = 8
Q_LORA = 384
KV_LORA = 256
NOPE_DIM = 64
ROPE_DIM = 32
V_DIM = 64
B_WIDTH = B_HEADS * V_DIM
ROPE_THETA = 10000.0
EPS = 1e-6
NEG_BIG = -1e30

LANES = 128
VMEM_LIMIT_BYTES = 48 * 1024 * 1024

C_QA = 0
C_KA = C_QA + A_WIDTH
C_VA = C_KA + 4 * LANES
C_ZA = C_VA + 4 * LANES
C_CQ = C_ZA + A_WIDTH
C_CKV = C_CQ + Q_LORA
C_KR = C_CKV + KV_LORA
C_KRS = C_KR + LANES
C_ZB = C_KRS + LANES
C_GM = C_ZB + B_WIDTH
C_END = C_GM + 2 * D_MODEL

BF16 = jnp.bfloat16
F32 = jnp.float32


def _silu(x):
    return x * (1.0 / (1.0 + jnp.exp(-x)))


def _sigmoid(x):
    return 1.0 / (1.0 + jnp.exp(-x))


def _dot(a, b):
    return jnp.dot(a, b, preferred_element_type=F32)


def _dot_nt(a, b):
    return lax.dot_general(a, b, (((1,), (1,)), ((), ())), preferred_element_type=F32)


def _mod_kernel(c_ref, w_ref, b_ref, o_ref):
    c = c_ref[...]
    o_ref[...] = jnp.dot(_silu(c), w_ref[...], preferred_element_type=F32,
                         precision=lax.Precision.HIGHEST) + b_ref[...]


def _mod_call(c_pad, w_ada, b_ada):
    rows = c_pad.shape[0]
    n = w_ada.shape[1]
    bn = 512
    return pl.pallas_call(
        _mod_kernel,
        out_shape=jax.ShapeDtypeStruct((rows, n), F32),
        grid=(n // bn,),
        in_specs=[
            pl.BlockSpec((rows, D_MODEL), lambda j: (0, 0)),
            pl.BlockSpec((D_MODEL, bn), lambda j: (0, j)),
            pl.BlockSpec((1, bn), lambda j: (0, j)),
        ],
        out_specs=pl.BlockSpec((rows, bn), lambda j: (0, j)),
        compiler_params=pltpu.CompilerParams(dimension_semantics=("arbitrary",),
                                             vmem_limit_bytes=VMEM_LIMIT_BYTES),
        name="mod",
    )(c_pad, w_ada, b_ada)


def _rms(x, gain):
    return x * lax.rsqrt(jnp.mean(x * x, axis=-1, keepdims=True) + EPS) * gain


def _inproj_kernel(x_ref, mod_ref, gn_ref, win_ref, gq_ref, wq_ref, wqs_ref, gkv_ref, wk_ref, wv_ref,
                   qcos_ref, qsin_ref, kcos_ref, ksin_ref,
                   qa_ref, ka_ref, va_ref, za_ref, qm_ref, km_ref, vm_ref, zb_ref, gm_ref):
    x = x_ref[0]
    mod = mod_ref[0]
    shift = mod[:, :D_MODEL]
    scale = mod[:, D_MODEL:2 * D_MODEL]
    h = (_rms(x, gn_ref[...]) * (1.0 + scale) + shift).astype(BF16)

    def seg(lo, width):
        return _dot(h, win_ref[:, lo:lo + width])

    qa_ref[0] = seg(C_QA, A_WIDTH).astype(BF16)
    ka_ref[0] = seg(C_KA, 4 * LANES).astype(BF16)
    va_ref[0] = seg(C_VA, 4 * LANES).astype(BF16)
    za_ref[0] = _silu(seg(C_ZA, A_WIDTH))
    zb_ref[0] = _silu(seg(C_ZB, B_WIDTH))
    gm_ref[0] = _sigmoid(seg(C_GM, 2 * D_MODEL))

    cqn = _rms(seg(C_CQ, Q_LORA), gq_ref[...]).astype(BF16)
    qcos = qcos_ref[...]
    qsin = qsin_ref[...]
    for hd in range(B_HEADS):
        sl = slice(hd * LANES, (hd + 1) * LANES)
        qh = _dot(cqn, wq_ref[:, sl]) * qcos + _dot(cqn, wqs_ref[:, sl]) * qsin
        qm_ref[0, :, sl] = qh.astype(BF16)

    ckvn = _rms(seg(C_CKV, KV_LORA), gkv_ref[...]).astype(BF16)
    kr = seg(C_KR, LANES) * kcos_ref[...] + seg(C_KRS, LANES) * ksin_ref[...]
    for hd in range(B_HEADS):
        sl = slice(hd * LANES, (hd + 1) * LANES)
        km_ref[0, :, sl] = (_dot(ckvn, wk_ref[:, sl]) + kr).astype(BF16)
    vm_ref[0, 0] = _dot_nt(wv_ref[...], ckvn).astype(BF16)


def _inproj_call(x, mod3, g_norm, w_in_x, g_q, wq, wqs, g_kv, wk, wv, tabs, tm):
    B, S, _ = x.shape
    nt = S // tm
    const2 = lambda b, i: (0, 0)
    tok = lambda w: pl.BlockSpec((1, tm, w), lambda b, i: (b, i, 0))
    tab = pl.BlockSpec((tm, LANES), lambda b, i: (i, 0))
    full = lambda a: pl.BlockSpec(a.shape, const2)
    out_widths = [(A_WIDTH, BF16), (4 * LANES, BF16), (4 * LANES, BF16), (A_WIDTH, F32),
                  (B_HEADS * LANES, BF16), (B_HEADS * LANES, BF16), None, (B_WIDTH, F32),
                  (2 * D_MODEL, F32)]
    out_shape = [jax.ShapeDtypeStruct((B, S, wd[0]), wd[1]) if wd else
                 jax.ShapeDtypeStruct((B, nt, B_WIDTH, tm), BF16) for wd in out_widths]
    out_specs = [tok(wd[0]) if wd else pl.BlockSpec((1, 1, B_WIDTH, tm), lambda b, i: (b, i, 0, 0))
                 for wd in out_widths]
    return pl.pallas_call(
        _inproj_kernel,
        out_shape=out_shape,
        grid=(B, nt),
        in_specs=[tok(D_MODEL),
                  pl.BlockSpec((1, 1, 3 * D_MODEL), lambda b, i: (b, 0, 0)),
                  full(g_norm), full(w_in_x), full(g_q), full(wq), full(wqs), full(g_kv), full(wk), full(wv),
                  tab, tab, tab, tab],
        out_specs=out_specs,
        compiler_params=pltpu.CompilerParams(dimension_semantics=("arbitrary", "arbitrary"),
                                             vmem_limit_bytes=VMEM_LIMIT_BYTES),
        name="inproj",
    )(x, mod3, g_norm, w_in_x, g_q, wq, wqs, g_kv, wk, wv, *tabs)


def _swa_kernel(sink_ref, q_ref, kp_ref, kc_ref, kn_ref, vp_ref, vc_ref, vn_ref, za_ref, o_ref, *, nsub, slopes):
    i = pl.program_id(1)
    last = pl.num_programs(1) - 1
    row = lax.broadcasted_iota(jnp.int32, (BLOCK, BLOCK), 0)
    col = lax.broadcasted_iota(jnp.int32, (BLOCK, BLOCK), 1)
    d = col - row
    dist = [jnp.abs(d - BLOCK).astype(F32), jnp.abs(d).astype(F32), jnp.abs(d + BLOCK).astype(F32)]
    far = 4 * BLOCK
    lo_edge = jnp.where(i > 0, 0, far)
    hi_edge = jnp.where(i < last, 0, -far)

    def kv_block(p_ref, c_ref, n_ref, jj, lanes):
        if jj < 0:
            return p_ref[0, :, lanes]
        if jj >= nsub:
            return n_ref[0, :, lanes]
        return c_ref[0, jj * BLOCK:(jj + 1) * BLOCK, lanes]

    for j in range(nsub):
        rows = slice(j * BLOCK, (j + 1) * BLOCK)
        valid = [d >= (lo_edge if j == 0 else 0), None, d <= (hi_edge if j == nsub - 1 else 0)]
        for pair in range(A_HEADS // 2):
            kvh = (2 * pair) // A_GROUP
            qp = q_ref[0, rows, pair * LANES:(pair + 1) * LANES]
            acc = None
            for e in range(2):
                head = 2 * pair + e
                lanes = slice((2 * kvh + e) * LANES, (2 * kvh + e + 1) * LANES)
                sink = sink_ref[head]
                s_blocks = []
                for t in range(3):
                    kb = kv_block(kp_ref, kc_ref, kn_ref, j - 1 + t, lanes)
                    s = _dot_nt(qp, kb) - slopes[head] * dist[t]
                    if valid[t] is not None:
                        s = jnp.where(valid[t], s, NEG_BIG)
                    s_blocks.append(s)
                m = jnp.maximum(jnp.maximum(s_blocks[0], s_blocks[1]), s_blocks[2])
                m = jnp.maximum(jnp.max(m, axis=-1, keepdims=True), sink)
                denom = jnp.exp(sink - m)
                o = None
                for t in range(3):
                    p = jnp.exp(s_blocks[t] - m)
                    denom = denom + jnp.sum(p, axis=-1, keepdims=True)
                    vb = kv_block(vp_ref, vc_ref, vn_ref, j - 1 + t, lanes)
                    pv = _dot(p.astype(BF16), vb)
                    o = pv if o is None else o + pv
                o = o * (1.0 / denom)
                acc = o if acc is None else acc + o
            gate = za_ref[0, rows, pair * LANES:(pair + 1) * LANES]
            o_ref[0, rows, pair * LANES:(pair + 1) * LANES] = (acc * gate).astype(BF16)


def _swa_call(sink, qa, ka, va, za, tq):
    B, S, _ = qa.shape
    nsub = tq // BLOCK
    nt = S // tq
    nblk = S // BLOCK
    slopes = tuple(float((2.0 ** (-8.0 / A_HEADS)) ** (k + 1)) for k in range(A_HEADS))
    cur = lambda w: pl.BlockSpec((1, tq, w), lambda b, i: (b, i, 0))
    prev = pl.BlockSpec((1, BLOCK, 4 * LANES), lambda b, i: (b, jnp.maximum(i * nsub - 1, 0), 0))
    nxt = pl.BlockSpec((1, BLOCK, 4 * LANES), lambda b, i: (b, jnp.minimum((i + 1) * nsub, nblk - 1), 0))
    return pl.pallas_call(
        functools.partial(_swa_kernel, nsub=nsub, slopes=slopes),
        out_shape=jax.ShapeDtypeStruct((B, S, A_WIDTH), BF16),
        grid=(B, nt),
        in_specs=[pl.BlockSpec(memory_space=pltpu.SMEM),
                  cur(A_WIDTH), prev, cur(4 * LANES), nxt, prev, cur(4 * LANES), nxt, cur(A_WIDTH)],
        out_specs=cur(A_WIDTH),
        compiler_params=pltpu.CompilerParams(dimension_semantics=("arbitrary", "arbitrary"),
                                             vmem_limit_bytes=VMEM_LIMIT_BYTES),
        name="swa",
    )(sink, qa, ka, ka, ka, va, va, va, za)


def _mla_kernel(q_ref, k_ref, v_ref, zb_ref, o_ref, s_ref, mu_ref, m_ref, l_ref, acc_ref, *, nk):
    m_ref[...] = jnp.full(m_ref.shape, NEG_BIG, F32)
    l_ref[...] = jnp.zeros(l_ref.shape, F32)
    acc_ref[...] = jnp.zeros(acc_ref.shape, F32)

    def scores(c, slot, e):
        lanes = slice(e * LANES, (e + 1) * LANES)
        s = _dot_nt(k_ref[0, c, :, lanes], q_ref[0, :, lanes])
        s_ref[slot, e] = s
        mu_ref[slot, e] = jnp.max(s, axis=0, keepdims=True)

    def update(c, slot, e):
        m_old = m_ref[e]
        m_new = jnp.maximum(m_old, mu_ref[slot, e])
        alpha = jnp.exp2(m_old - m_new)
        p = jnp.exp2(s_ref[slot, e] - m_new)
        l_ref[e] = alpha * l_ref[e] + jnp.sum(p, axis=0, keepdims=True)
        vt = v_ref[0, c, e * V_DIM:(e + 1) * V_DIM, :]
        acc_ref[e] = alpha * acc_ref[e] + _dot(vt, p.astype(BF16))
        m_ref[e] = m_new

    def step(c_next, c, slot):
        for e in range(2):
            scores(c_next, 1 - slot, e)
            update(c, slot, e)

    for e in range(2):
        scores(0, 0, e)

    def body(j, carry):
        c = 2 * j
        step(c + 1, c, 0)
        step(c + 2, c + 1, 1)
        return carry

    lax.fori_loop(0, nk // 2 - 1, body, 0)
    step(nk - 1, nk - 2, 0)
    for e in range(2):
        update(nk - 1, 1, e)

    ot = jnp.concatenate([acc_ref[0] * (1.0 / l_ref[0]), acc_ref[1] * (1.0 / l_ref[1])], axis=0)
    o_ref[0] = (ot.T * zb_ref[0]).astype(BF16)


def _mla_call(qm, km, vm, zb, bq):
    B, S, _ = qm.shape
    _, nk, _, bk = vm.shape
    npair = B_HEADS // 2
    nq = S // bq
    km = km.reshape(B, nk, bk, B_HEADS * LANES)
    return pl.pallas_call(
        functools.partial(_mla_kernel, nk=nk),
        out_shape=jax.ShapeDtypeStruct((B, S, B_WIDTH), BF16),
        grid=(B, npair, nq),
        in_specs=[pl.BlockSpec((1, bq, 2 * LANES), lambda b, p, i: (b, i, p)),
                  pl.BlockSpec((1, nk, bk, 2 * LANES), lambda b, p, i: (b, 0, 0, p)),
                  pl.BlockSpec((1, nk, 2 * V_DIM, bk), lambda b, p, i: (b, 0, p, 0)),
                  pl.BlockSpec((1, bq, LANES), lambda b, p, i: (b, i, p))],
        out_specs=pl.BlockSpec((1, bq, LANES), lambda b, p, i: (b, i, p)),
        scratch_shapes=[pltpu.VMEM((2, 2, bk, bq), F32), pltpu.VMEM((2, 2, 1, bq), F32),
                        pltpu.VMEM((2, 1, bq), F32), pltpu.VMEM((2, 1, bq), F32),
                        pltpu.VMEM((2, V_DIM, bq), F32)],
        compiler_params=pltpu.CompilerParams(dimension_semantics=("arbitrary", "arbitrary", "arbitrary"),
                                             vmem_limit_bytes=VMEM_LIMIT_BYTES),
        name="mla",
    )(qm, km, vm, zb)


def _outproj_kernel(x_ref, mod_ref, ya_ref, yb_ref, gm_ref, woa_ref, wob_ref, wout_ref, gf_ref, o_ref, *, final):
    ga = gm_ref[0, :, :D_MODEL]
    gb = gm_ref[0, :, D_MODEL:]
    merged = ga * _dot(ya_ref[0], woa_ref[...]) + gb * _dot(yb_ref[0], wob_ref[...])
    gate_res = mod_ref[0][:, 2 * D_MODEL:]
    y = x_ref[0] + gate_res * _dot(merged.astype(BF16), wout_ref[...])
    if final:
        y = _rms(y, gf_ref[...])
    o_ref[0] = y


def _outproj_call(x, mod3, ya, yb, gm, w_oa, w_ob, w_out, g_final, tm, final):
    B, S, _ = x.shape
    nt = S // tm
    const2 = lambda b, i: (0, 0)
    tok = lambda w: pl.BlockSpec((1, tm, w), lambda b, i: (b, i, 0))
    full = lambda a: pl.BlockSpec(a.shape, const2)
    return pl.pallas_call(
        functools.partial(_outproj_kernel, final=final),
        out_shape=jax.ShapeDtypeStruct((B, S, D_MODEL), F32),
        grid=(B, nt),
        in_specs=[tok(D_MODEL),
                  pl.BlockSpec((1, 1, 3 * D_MODEL), lambda b, i: (b, 0, 0)),
                  tok(A_WIDTH), tok(B_WIDTH), tok(2 * D_MODEL),
                  full(w_oa), full(w_ob), full(w_out), full(g_final)],
        out_specs=tok(D_MODEL),
        compiler_params=pltpu.CompilerParams(dimension_semantics=("arbitrary", "arbitrary"),
                                             vmem_limit_bytes=VMEM_LIMIT_BYTES),
        name="outproj",
    )(x, mod3, ya, yb, gm, w_oa, w_ob, w_out, g_final)


def _pair_ext(w):
    z = jnp.zeros((w.shape[0], HEAD_DIM), w.dtype)
    groups = []
    for kvh in range(A_KV_HEADS):
        wk = w[:, kvh * HEAD_DIM:(kvh + 1) * HEAD_DIM]
        groups += [wk, z, z, wk]
    return jnp.concatenate(groups, axis=1)


def _swap_halves(w):
    half = w.shape[-1] // 2
    return jnp.concatenate([w[..., half:], w[..., :half]], axis=-1)


def _place_rope(w):
    d = w.shape[0]
    return jnp.concatenate([jnp.zeros((d, NOPE_DIM), w.dtype), w, jnp.zeros((d, LANES - NOPE_DIM - ROPE_DIM), w.dtype)], axis=1)


def _prep_layer(w_in, w_uq, w_ukv):
    o = 0
    parts = {}
    for name, width in (("qa", A_WIDTH), ("ka", A_KV_WIDTH), ("va", A_KV_WIDTH), ("za", A_WIDTH), ("cq", Q_LORA),
                        ("ckv", KV_LORA), ("kr", ROPE_DIM), ("zb", B_WIDTH), ("gm", 2 * D_MODEL)):
        parts[name] = w_in[:, o:o + width]
        o += width
    w_in_x = jnp.concatenate([
        parts["qa"] * (HEAD_DIM ** -0.5), _pair_ext(parts["ka"]), _pair_ext(parts["va"]), parts["za"],
        parts["cq"], parts["ckv"], _place_rope(parts["kr"]), _place_rope(_swap_halves(parts["kr"])),
        parts["zb"], parts["gm"]], axis=1).astype(BF16)
    assert w_in_x.shape[1] == C_END

    uq = w_uq.reshape(Q_LORA, B_HEADS, NOPE_DIM + ROPE_DIM)
    zq = jnp.zeros((Q_LORA, B_HEADS, LANES - NOPE_DIM - ROPE_DIM), w_uq.dtype)
    wq = jnp.concatenate([uq, zq], axis=-1).reshape(Q_LORA, B_HEADS * LANES).astype(BF16)
    wqs = jnp.concatenate([jnp.zeros((Q_LORA, B_HEADS, NOPE_DIM), w_uq.dtype), _swap_halves(uq[..., NOPE_DIM:]), zq],
                          axis=-1).reshape(Q_LORA, B_HEADS * LANES).astype(BF16)
    ukv = w_ukv.reshape(KV_LORA, B_HEADS, NOPE_DIM + V_DIM)
    wk = jnp.concatenate([ukv[..., :NOPE_DIM], jnp.zeros((KV_LORA, B_HEADS, LANES - NOPE_DIM), w_ukv.dtype)],
                         axis=-1).reshape(KV_LORA, B_HEADS * LANES).astype(BF16)
    wv = ukv[..., NOPE_DIM:].reshape(KV_LORA, B_WIDTH).T.astype(BF16)
    return w_in_x, wq, wqs, wk, wv


def _rope_tables(S):
    inv = 1.0 / (ROPE_THETA ** (jnp.arange(0, ROPE_DIM, 2, dtype=F32) / ROPE_DIM))
    ang = jnp.arange(S, dtype=F32)[:, None] * inv[None, :]
    cos, sin = jnp.cos(ang), jnp.sin(ang)
    ones = jnp.ones((S, NOPE_DIM), F32)
    zeros_n = jnp.zeros((S, NOPE_DIM), F32)
    zeros_p = jnp.zeros((S, LANES - NOPE_DIM - ROPE_DIM), F32)
    cos_t = jnp.concatenate([ones, cos, cos, zeros_p], axis=1)
    sin_t = jnp.concatenate([zeros_n, -sin, sin, zeros_p], axis=1)
    qscale = (NOPE_DIM + ROPE_DIM) ** -0.5 * math.log2(math.e)
    return cos_t * qscale, sin_t * qscale, cos_t, sin_t


def _trunk(x, c, layers, g_final, mod_all, row0):
    B, S, _ = x.shape
    tabs = _rope_tables(S)
    tm = 256
    n_layers = len(layers)
    for l, lw in enumerate(layers):
        mod3 = mod_all[l][row0:row0 + B].reshape(B, 1, 3 * D_MODEL)
        qa, ka, va, za, qm, km, vm, zb, gm = _inproj_call(
            x, mod3, lw["g_norm"], lw["w_in_x"], lw["g_q"], lw["wq"], lw["wqs"], lw["g_kv"], lw["wk"], lw["wv"],
            tabs, tm)
        ya = _swa_call(lw["sink"], qa, ka, va, za, 512)
        yb = _mla_call(qm, km, vm, zb, 512)
        x = _outproj_call(x, mod3, ya, yb, gm, lw["w_oa"], lw["w_ob"], lw["w_out"], g_final, tm,
                          final=(l == n_layers - 1))
    return x


def kernel(x_prompt, x_sample, c_prompt, c_sample, w_ada, b_ada, g_norm, w_in, g_q, w_uq, g_kv, w_ukv, sink,
           w_oa, w_ob, w_out, g_final):
    depth = w_in.shape[0]
    nb_p, nb_s = c_prompt.shape[0], c_sample.shape[0]
    rows = -(-(nb_p + nb_s) // 8) * 8
    c_pad = jnp.concatenate([c_prompt, c_sample, jnp.zeros((rows - nb_p - nb_s, D_MODEL), F32)], axis=0)
    layers, mod_all = [], []
    for l in range(depth):
        w_in_x, wq, wqs, wk, wv = _prep_layer(w_in[l], w_uq[l], w_ukv[l])
        layers.append(dict(
            g_norm=g_norm[l].reshape(1, D_MODEL), w_in_x=w_in_x, g_q=g_q[l].reshape(1, Q_LORA), wq=wq, wqs=wqs,
            g_kv=g_kv[l].reshape(1, KV_LORA), wk=wk, wv=wv, sink=sink[l],
            w_oa=w_oa[l].astype(BF16), w_ob=w_ob[l].astype(BF16), w_out=w_out[l].astype(BF16)))
        mod_all.append(_mod_call(c_pad, w_ada[l], b_ada[l].reshape(1, 3 * D_MODEL)))
    gf = g_final.reshape(1, D_MODEL)
    y_prompt = _trunk(x_prompt, c_prompt, layers, gf, mod_all, 0)
    y_sample = _trunk(x_sample, c_sample, layers, gf, mod_all, nb_p)
    return (y_prompt, y_sample)
```

```python
import functools
import math

import jax
import jax.numpy as jnp
from jax import lax
from jax.experimental import pallas as pl
from jax.experimental.pallas import tpu as pltpu

D_MODEL = 1024
HEAD_DIM = 64
A_HEADS = 8
A_KV_HEADS = 2
A_GROUP = A_HEADS // A_KV_HEADS
A_WIDTH = A_HEADS * HEAD_DIM
A_KV_WIDTH = A_KV_HEADS * HEAD_DIM
WINDOW = 128
BLOCK = 128
B_HEADS = 8
Q_LORA = 384
KV_LORA = 256
NOPE_DIM = 64
ROPE_DIM = 32
V_DIM = 64
B_WIDTH = B_HEADS * V_DIM
ROPE_THETA = 10000.0
EPS = 1e-6
NEG_BIG = -1e30

LANES = 128
VMEM_LIMIT_BYTES = 48 * 1024 * 1024
VMEM_LIMIT_INPROJ_BYTES = 56 * 1024 * 1024
BF16_SUBLANES = 16
VT_ROWS = 64 + BF16_SUBLANES

C_QA = 0
C_KA = C_QA + A_HEADS * LANES
C_ZA = C_KA + A_KV_WIDTH
C_CQ = C_ZA + A_WIDTH
C_CKV = C_CQ + Q_LORA
C_KR = C_CKV + KV_LORA
C_KRS = C_KR + LANES
C_ZB = C_KRS + LANES
C_GM = C_ZB + B_WIDTH
C_END = C_GM + 2 * D_MODEL

BF16 = jnp.bfloat16
F32 = jnp.float32


def _silu(x):
    return x * (1.0 / (1.0 + jnp.exp(-x)))


def _sigmoid(x):
    return 1.0 / (1.0 + jnp.exp(-x))


def _dot(a, b):
    return jnp.dot(a, b, preferred_element_type=F32)


def _dot_nt(a, b):
    return lax.dot_general(a, b, (((1,), (1,)), ((), ())), preferred_element_type=F32)


def _mod_kernel(c_ref, w_ref, b_ref, o_ref):
    c = c_ref[...]
    o_ref[...] = jnp.dot(_silu(c), w_ref[...], preferred_element_type=F32,
                         precision=lax.Precision.HIGHEST) + b_ref[...]


def _mod_call(c_pad, w_ada, b_ada):
    rows = c_pad.shape[0]
    n = w_ada.shape[1]
    bn = 512
    return pl.pallas_call(
        _mod_kernel,
        out_shape=jax.ShapeDtypeStruct((rows, n), F32),
        grid=(n // bn,),
        in_specs=[
            pl.BlockSpec((rows, D_MODEL), lambda j: (0, 0)),
            pl.BlockSpec((D_MODEL, bn), lambda j: (0, j)),
            pl.BlockSpec((1, bn), lambda j: (0, j)),
        ],
        out_specs=pl.BlockSpec((rows, bn), lambda j: (0, j)),
        compiler_params=pltpu.CompilerParams(dimension_semantics=("arbitrary",),
                                             vmem_limit_bytes=VMEM_LIMIT_BYTES),
        name="mod",
    )(c_pad, w_ada, b_ada)


def _rms(x, gain):
    return x * lax.rsqrt(jnp.mean(x * x, axis=-1, keepdims=True) + EPS) * gain


def _inproj_kernel(x_ref, mod_ref, gn_ref, win_ref, wvat_ref, gq_ref, wq_ref, wqs_ref, gkv_ref, wk_ref, wv_ref,
                   qcos_ref, qsin_ref, kcos_ref, ksin_ref,
                   qa_ref, ka_ref, va_ref, za_ref, qm_ref, km_ref, vm_ref, zb_ref, gm_ref):
    x = x_ref[0]
    mod = mod_ref[0]
    shift = mod[:, :D_MODEL]
    scale = mod[:, D_MODEL:2 * D_MODEL]
    h = (_rms(x, gn_ref[...]) * (1.0 + scale) + shift).astype(BF16)

    def seg(lo, width):
        return _dot(h, win_ref[:, lo:lo + width])

    qa_ref[0] = seg(C_QA, A_HEADS * LANES).astype(BF16)
    ka_ref[0] = seg(C_KA, A_KV_WIDTH).astype(BF16)
    va_ref[0] = _dot_nt(wvat_ref[...], h).astype(BF16)
    za_ref[0] = _silu(seg(C_ZA, A_WIDTH))
    zb_ref[0] = _silu(seg(C_ZB, B_WIDTH))
    gm_ref[0] = _sigmoid(seg(C_GM, 2 * D_MODEL))

    lat = seg(C_CQ, C_ZB - C_CQ)
    cq = lat[:, :Q_LORA]
    ckv = lat[:, C_CKV - C_CQ:C_KR - C_CQ]
    kr_a = lat[:, C_KR - C_CQ:C_KRS - C_CQ]
    kr_b = lat[:, C_KRS - C_CQ:]

    cqn = _rms(cq, gq_ref[...]).astype(BF16)
    qcos = jnp.concatenate([qcos_ref[...]] * 2, axis=1)
    qsin = jnp.concatenate([qsin_ref[...]] * 2, axis=1)
    for pair in range(B_HEADS // 2):
        sl = slice(pair * 2 * LANES, (pair + 1) * 2 * LANES)
        qh = _dot(cqn, wq_ref[:, sl]) * qcos + _dot(cqn, wqs_ref[:, sl]) * qsin
        qm_ref[0, :, sl] = qh.astype(BF16)

    ckvn = _rms(ckv, gkv_ref[...]).astype(BF16)
    kr = kr_a * kcos_ref[...] + kr_b * ksin_ref[...]
    kr = jnp.concatenate([kr, kr], axis=1)
    for pair in range(B_HEADS // 2):
        sl = slice(pair * 2 * LANES, (pair + 1) * 2 * LANES)
        km_ref[0, :, sl] = (_dot(ckvn, wk_ref[:, sl]) + kr).astype(BF16)
    bk = vm_ref.shape[-1]
    row = lax.broadcasted_iota(jnp.int32, (B_HEADS * VT_ROWS, bk), 0)
    is_one = row % VT_ROWS >= V_DIM
    for j in range(vm_ref.shape[1]):
        vt = _dot_nt(wv_ref[...], ckvn[j * bk:(j + 1) * bk])
        vm_ref[0, j] = jnp.where(is_one, 1.0, vt).astype(BF16)


def _inproj_call(x, mod3, g_norm, w_in_x, w_vat, g_q, wq, wqs, g_kv, wk, wv, tabs, tm, bk):
    B, S, _ = x.shape
    nt = S // tm
    nsub = tm // bk
    const2 = lambda b, i: (0, 0)
    tok = lambda w: pl.BlockSpec((1, tm, w), lambda b, i: (b, i, 0))
    tab = pl.BlockSpec((tm, LANES), lambda b, i: (i, 0))
    full = lambda a: pl.BlockSpec(a.shape, const2)
    outs = [((B, S, A_HEADS * LANES), BF16, tok(A_HEADS * LANES)),
            ((B, S, A_KV_WIDTH), BF16, tok(A_KV_WIDTH)),
            ((B, A_KV_WIDTH, S), BF16, pl.BlockSpec((1, A_KV_WIDTH, tm), lambda b, i: (b, 0, i))),
            ((B, S, A_WIDTH), F32, tok(A_WIDTH)),
            ((B, S, B_HEADS * LANES), BF16, tok(B_HEADS * LANES)),
            ((B, S, B_HEADS * LANES), BF16, tok(B_HEADS * LANES)),
            ((B, nt * nsub, B_HEADS * VT_ROWS, bk), BF16,
             pl.BlockSpec((1, nsub, B_HEADS * VT_ROWS, bk), lambda b, i: (b, i, 0, 0))),
            ((B, S, B_WIDTH), F32, tok(B_WIDTH)),
            ((B, S, 2 * D_MODEL), F32, tok(2 * D_MODEL))]
    out_shape = [jax.ShapeDtypeStruct(shape, dt) for shape, dt, _ in outs]
    out_specs = [spec for _, _, spec in outs]
    return pl.pallas_call(
        _inproj_kernel,
        out_shape=out_shape,
        grid=(B, nt),
        in_specs=[tok(D_MODEL),
                  pl.BlockSpec((1, 1, 3 * D_MODEL), lambda b, i: (b, 0, 0)),
                  full(g_norm), full(w_in_x), full(w_vat), full(g_q), full(wq), full(wqs), full(g_kv), full(wk),
                  full(wv), tab, tab, tab, tab],
        out_specs=out_specs,
        compiler_params=pltpu.CompilerParams(dimension_semantics=("arbitrary", "arbitrary"),
                                             vmem_limit_bytes=VMEM_LIMIT_INPROJ_BYTES),
        name="inproj",
    )(x, mod3, g_norm, w_in_x, w_vat, g_q, wq, wqs, g_kv, wk, wv, *tabs)


def _swa_kernel(sink_ref, q_ref, kp_ref, kc_ref, kn_ref, vp_ref, vc_ref, vn_ref, za_ref, o_ref, bias_ref,
                *, nsub, slopes):
    i = pl.program_id(1)
    last = pl.num_programs(1) - 1

    @pl.when(i == 0)
    def _():
        row = lax.broadcasted_iota(jnp.int32, (BLOCK, BLOCK), 0)
        col = lax.broadcasted_iota(jnp.int32, (BLOCK, BLOCK), 1)
        for t in range(3):
            rel = row - col + (t - 1) * BLOCK
            dist = jnp.abs(rel).astype(F32)
            valid = jnp.abs(rel) <= WINDOW
            for head in range(A_HEADS):
                bias_ref[head, t] = jnp.where(valid, -slopes[head] * dist, NEG_BIG)

    edge_lo = jnp.where(i > 0, 0.0, NEG_BIG).astype(F32)
    edge_hi = jnp.where(i < last, 0.0, NEG_BIG).astype(F32)
    lane2 = lax.broadcasted_iota(jnp.int32, (1, 2 * BLOCK), 1)

    def k_block(jj):
        if jj < 0:
            return kp_ref[0]
        if jj >= nsub:
            return kn_ref[0]
        return kc_ref[0, jj * BLOCK:(jj + 1) * BLOCK, :]

    def vt_block(jj, vrows):
        if jj < 0:
            return vp_ref[0, vrows, :]
        if jj >= nsub:
            return vn_ref[0, vrows, :]
        return vc_ref[0, vrows, jj * BLOCK:(jj + 1) * BLOCK]

    def scores(unit):
        j, pair = unit
        rows = slice(j * BLOCK, (j + 1) * BLOCK)
        q2 = jnp.concatenate([q_ref[0, rows, (2 * pair + e) * LANES:(2 * pair + e + 1) * LANES] for e in range(2)],
                             axis=0)
        out = []
        for t in range(3):
            s = _dot_nt(k_block(j - 1 + t), q2)
            s = s + jnp.concatenate([bias_ref[2 * pair, t], bias_ref[2 * pair + 1, t]], axis=1)
            if j == 0 and t == 0:
                s = s + edge_lo
            if j == nsub - 1 and t == 2:
                s = s + edge_hi
            out.append(s)
        return out

    def finish(unit, s_blocks):
        j, pair = unit
        rows = slice(j * BLOCK, (j + 1) * BLOCK)
        kvh = (2 * pair) // A_GROUP
        vrows = slice(kvh * HEAD_DIM, (kvh + 1) * HEAD_DIM)
        sink = jnp.where(lane2 < BLOCK, sink_ref[2 * pair], sink_ref[2 * pair + 1])
        m = jnp.maximum(jnp.maximum(s_blocks[0], s_blocks[1]), s_blocks[2])
        m = jnp.maximum(jnp.max(m, axis=0, keepdims=True), sink)
        denom = jnp.exp(sink - m)
        ps = []
        for t in range(3):
            p = jnp.exp(s_blocks[t] - m)
            denom = denom + jnp.sum(p, axis=0, keepdims=True)
            ps.append(p.astype(BF16))
        vt = jnp.concatenate([vt_block(j - 1 + t, vrows) for t in range(3)], axis=1)
        ot = _dot(vt, jnp.concatenate(ps, axis=0)) * (1.0 / denom)
        o2 = jnp.concatenate([ot[:, :BLOCK], ot[:, BLOCK:]], axis=0)
        gate = za_ref[0, rows, pair * LANES:(pair + 1) * LANES]
        o_ref[0, rows, pair * LANES:(pair + 1) * LANES] = (o2.T * gate).astype(BF16)

    units = [(j, pair) for j in range(nsub) for pair in range(A_HEADS // 2)]
    pending = scores(units[0])
    for idx, unit in enumerate(units):
        ahead = scores(units[idx + 1]) if idx + 1 < len(units) else None
        finish(unit, pending)
        pending = ahead


def _swa_call(sink, qa, ka, vat, za, tq):
    B, S, _ = qa.shape
    nsub = tq // BLOCK
    nt = S // tq
    nblk = S // BLOCK
    slopes = tuple(float((2.0 ** (-8.0 / A_HEADS)) ** (k + 1)) for k in range(A_HEADS))
    cur = lambda w: pl.BlockSpec((1, tq, w), lambda b, i: (b, i, 0))
    prev_blk = lambda i: jnp.maximum(i * nsub - 1, 0)
    next_blk = lambda i: jnp.minimum((i + 1) * nsub, nblk - 1)
    k_prev = pl.BlockSpec((1, BLOCK, LANES), lambda b, i: (b, prev_blk(i), 0))
    k_next = pl.BlockSpec((1, BLOCK, LANES), lambda b, i: (b, next_blk(i), 0))
    v_prev = pl.BlockSpec((1, A_KV_WIDTH, BLOCK), lambda b, i: (b, 0, prev_blk(i)))
    v_cur = pl.BlockSpec((1, A_KV_WIDTH, tq), lambda b, i: (b, 0, i))
    v_next = pl.BlockSpec((1, A_KV_WIDTH, BLOCK), lambda b, i: (b, 0, next_blk(i)))
    return pl.pallas_call(
        functools.partial(_swa_kernel, nsub=nsub, slopes=slopes),
        out_shape=jax.ShapeDtypeStruct((B, S, A_WIDTH), BF16),
        grid=(B, nt),
        in_specs=[pl.BlockSpec(memory_space=pltpu.SMEM),
                  cur(A_HEADS * LANES), k_prev, cur(LANES), k_next, v_prev, v_cur, v_next, cur(A_WIDTH)],
        out_specs=cur(A_WIDTH),
        scratch_shapes=[pltpu.VMEM((A_HEADS, 3, BLOCK, BLOCK), F32)],
        compiler_params=pltpu.CompilerParams(dimension_semantics=("arbitrary", "arbitrary"),
                                             vmem_limit_bytes=VMEM_LIMIT_BYTES),
        name="swa",
    )(sink, qa, ka, ka, ka, vat, vat, vat, za)


def _mla_kernel(q_ref, k_ref, v_ref, zb_ref, o_ref, s_ref, mu_ref, m_ref, acc_ref, *, nk, unroll):
    m_ref[...] = jnp.full(m_ref.shape, NEG_BIG, F32)
    acc_ref[...] = jnp.zeros(acc_ref.shape, F32)

    def scores(c, slot, e):
        lanes = slice(e * LANES, (e + 1) * LANES)
        s = _dot_nt(k_ref[0, c, :, lanes], q_ref[0, :, lanes])
        s_ref[slot, e] = s
        mu_ref[slot, e] = jnp.max(s, axis=0, keepdims=True)

    def update(c, slot, e):
        m_old = m_ref[e]
        m_new = jnp.maximum(m_old, mu_ref[slot, e])
        alpha = jnp.exp2(m_old - m_new)
        p = jnp.exp2(s_ref[slot, e] - m_new)
        vt = v_ref[0, c, e * VT_ROWS:(e + 1) * VT_ROWS, :]
        acc_ref[e] = alpha * acc_ref[e] + _dot(vt, p.astype(BF16))
        m_ref[e] = m_new

    def step(c_next, c, slot):
        for e in range(2):
            scores(c_next, 1 - slot, e)
            update(c, slot, e)

    for e in range(2):
        scores(0, 0, e)

    def body(j, carry):
        c = unroll * j
        for u in range(unroll):
            step(c + u + 1, c + u, u % 2)
        return carry

    lax.fori_loop(0, nk // unroll - 1, body, 0)
    for u in range(unroll - 1):
        step(nk - unroll + u + 1, nk - unroll + u, u % 2)
    for e in range(2):
        update(nk - 1, (unroll - 1) % 2, e)

    ot = jnp.concatenate([acc_ref[e, :V_DIM] * (1.0 / acc_ref[e, V_DIM:V_DIM + 1]) for e in range(2)],
                         axis=0)
    o_ref[0] = (ot.T * zb_ref[0]).astype(BF16)


def _mla_call(qm, km, vm, zb, bq):
    B, S, _ = qm.shape
    _, nk, _, bk = vm.shape
    npair = B_HEADS // 2
    nq = S // bq
    km = km.reshape(B, nk, bk, B_HEADS * LANES)
    unroll = 8
    assert nk % unroll == 0
    return pl.pallas_call(
        functools.partial(_mla_kernel, nk=nk, unroll=unroll),
        out_shape=jax.ShapeDtypeStruct((B, S, B_WIDTH), BF16),
        grid=(B, npair, nq),
        in_specs=[pl.BlockSpec((1, bq, 2 * LANES), lambda b, p, i: (b, i, p)),
                  pl.BlockSpec((1, nk, bk, 2 * LANES), lambda b, p, i: (b, 0, 0, p)),
                  pl.BlockSpec((1, nk, 2 * VT_ROWS, bk), lambda b, p, i: (b, 0, p, 0)),
                  pl.BlockSpec((1, bq, LANES), lambda b, p, i: (b, i, p))],
        out_specs=pl.BlockSpec((1, bq, LANES), lambda b, p, i: (b, i, p)),
        scratch_shapes=[pltpu.VMEM((2, 2, bk, bq), F32), pltpu.VMEM((2, 2, 1, bq), F32),
                        pltpu.VMEM((2, 1, bq), F32), pltpu.VMEM((2, VT_ROWS, bq), F32)],
        compiler_params=pltpu.CompilerParams(dimension_semantics=("arbitrary", "arbitrary", "arbitrary"),
                                             vmem_limit_bytes=VMEM_LIMIT_BYTES),
        name="mla",
    )(qm, km, vm, zb)


def _outproj_kernel(x_ref, mod_ref, ya_ref, yb_ref, gm_ref, woa_ref, wob_ref, wout_ref, gf_ref, o_ref, *, final):
    ga = gm_ref[0, :, :D_MODEL]
    gb = gm_ref[0, :, D_MODEL:]
    merged = ga * _dot(ya_ref[0], woa_ref[...]) + gb * _dot(yb_ref[0], wob_ref[...])
    gate_res = mod_ref[0][:, 2 * D_MODEL:]
    y = x_ref[0] + gate_res * _dot(merged.astype(BF16), wout_ref[...])
    if final:
        y = _rms(y, gf_ref[...])
    o_ref[0] = y


def _outproj_call(x, mod3, ya, yb, gm, w_oa, w_ob, w_out, g_final, tm, final):
    B, S, _ = x.shape
    nt = S // tm
    const2 = lambda b, i: (0, 0)
    tok = lambda w: pl.BlockSpec((1, tm, w), lambda b, i: (b, i, 0))
    full = lambda a: pl.BlockSpec(a.shape, const2)
    return pl.pallas_call(
        functools.partial(_outproj_kernel, final=final),
        out_shape=jax.ShapeDtypeStruct((B, S, D_MODEL), F32),
        grid=(B, nt),
        in_specs=[tok(D_MODEL),
                  pl.BlockSpec((1, 1, 3 * D_MODEL), lambda b, i: (b, 0, 0)),
                  tok(A_WIDTH), tok(B_WIDTH), tok(2 * D_MODEL),
                  full(w_oa), full(w_ob), full(w_out), full(g_final)],
        out_specs=tok(D_MODEL),
        compiler_params=pltpu.CompilerParams(dimension_semantics=("arbitrary", "arbitrary"),
                                             vmem_limit_bytes=VMEM_LIMIT_BYTES),
        name="outproj",
    )(x, mod3, ya, yb, gm, w_oa, w_ob, w_out, g_final)


def _head_groups(w):
    z = jnp.zeros((w.shape[0], HEAD_DIM), w.dtype)
    groups = []
    for head in range(A_HEADS):
        wh = w[:, head * HEAD_DIM:(head + 1) * HEAD_DIM]
        groups += [wh, z] if head // A_GROUP == 0 else [z, wh]
    return jnp.concatenate(groups, axis=1)


def _swap_halves(w):
    half = w.shape[-1] // 2
    return jnp.concatenate([w[..., half:], w[..., :half]], axis=-1)


def _place_rope(w):
    d = w.shape[0]
    return jnp.concatenate([jnp.zeros((d, NOPE_DIM), w.dtype), w, jnp.zeros((d, LANES - NOPE_DIM - ROPE_DIM), w.dtype)], axis=1)


def _prep_layer(w_in, w_uq, w_ukv):
    o = 0
    parts = {}
    for name, width in (("qa", A_WIDTH), ("ka", A_KV_WIDTH), ("va", A_KV_WIDTH), ("za", A_WIDTH), ("cq", Q_LORA),
                        ("ckv", KV_LORA), ("kr", ROPE_DIM), ("zb", B_WIDTH), ("gm", 2 * D_MODEL)):
        parts[name] = w_in[:, o:o + width]
        o += width
    w_in_x = jnp.concatenate([
        _head_groups(parts["qa"] * (HEAD_DIM ** -0.5)), parts["ka"], parts["za"],
        parts["cq"], parts["ckv"], _place_rope(parts["kr"]), _place_rope(_swap_halves(parts["kr"])),
        parts["zb"], parts["gm"]], axis=1).astype(BF16)
    assert w_in_x.shape[1] == C_END
    w_vat = parts["va"].T.astype(BF16)

    uq = w_uq.reshape(Q_LORA, B_HEADS, NOPE_DIM + ROPE_DIM)
    zq = jnp.zeros((Q_LORA, B_HEADS, LANES - NOPE_DIM - ROPE_DIM), w_uq.dtype)
    wq = jnp.concatenate([uq, zq], axis=-1).reshape(Q_LORA, B_HEADS * LANES).astype(BF16)
    wqs = jnp.concatenate([jnp.zeros((Q_LORA, B_HEADS, NOPE_DIM), w_uq.dtype), _swap_halves(uq[..., NOPE_DIM:]), zq],
                          axis=-1).reshape(Q_LORA, B_HEADS * LANES).astype(BF16)
    ukv = w_ukv.reshape(KV_LORA, B_HEADS, NOPE_DIM + V_DIM)
    wk = jnp.concatenate([ukv[..., :NOPE_DIM], jnp.zeros((KV_LORA, B_HEADS, LANES - NOPE_DIM), w_ukv.dtype)],
                         axis=-1).reshape(KV_LORA, B_HEADS * LANES).astype(BF16)
    wv = jnp.transpose(ukv[..., NOPE_DIM:], (1, 2, 0))
    wv = jnp.concatenate([wv, jnp.zeros((B_HEADS, VT_ROWS - V_DIM, KV_LORA), w_ukv.dtype)], axis=1)
    wv = wv.reshape(B_HEADS * VT_ROWS, KV_LORA).astype(BF16)
    return w_in_x, w_vat, wq, wqs, wk, wv


def _rope_tables(S):
    inv = 1.0 / (ROPE_THETA ** (jnp.arange(0, ROPE_DIM, 2, dtype=F32) / ROPE_DIM))
    ang = jnp.arange(S, dtype=F32)[:, None] * inv[None, :]
    cos, sin = jnp.cos(ang), jnp.sin(ang)
    ones = jnp.ones((S, NOPE_DIM), F32)
    zeros_n = jnp.zeros((S, NOPE_DIM), F32)
    zeros_p = jnp.zeros((S, LANES - NOPE_DIM - ROPE_DIM), F32)
    cos_t = jnp.concatenate([ones, cos, cos, zeros_p], axis=1)
    sin_t = jnp.concatenate([zeros_n, -sin, sin, zeros_p], axis=1)
    qscale = (NOPE_DIM + ROPE_DIM) ** -0.5 * math.log2(math.e)
    return cos_t * qscale, sin_t * qscale, cos_t, sin_t


TILE_PROJ = 512
TILE_SWA = 512
TILE_MLA_Q = 512
TILE_MLA_K = 256


def _trunk(x, layers, g_final, mod_all, row0):
    B, S, _ = x.shape
    tabs = _rope_tables(S)
    n_layers = len(layers)
    for l, lw in enumerate(layers):
        mod3 = mod_all[l][row0:row0 + B].reshape(B, 1, 3 * D_MODEL)
        qa, ka, va, za, qm, km, vm, zb, gm = _inproj_call(
            x, mod3, lw["g_norm"], lw["w_in_x"], lw["w_vat"], lw["g_q"], lw["wq"], lw["wqs"], lw["g_kv"], lw["wk"],
            lw["wv"],
            tabs, TILE_PROJ, TILE_MLA_K)
        ya = _swa_call(lw["sink"], qa, ka, va, za, TILE_SWA)
        yb = _mla_call(qm, km, vm, zb, TILE_MLA_Q)
        x = _outproj_call(x, mod3, ya, yb, gm, lw["w_oa"], lw["w_ob"], lw["w_out"], g_final, TILE_PROJ,
                          final=(l == n_layers - 1))
    return x


def kernel(x_prompt, x_sample, c_prompt, c_sample, w_ada, b_ada, g_norm, w_in, g_q, w_uq, g_kv, w_ukv, sink,
           w_oa, w_ob, w_out, g_final):
    depth = w_in.shape[0]
    nb_p, nb_s = c_prompt.shape[0], c_sample.shape[0]
    rows = -(-(nb_p + nb_s) // 8) * 8
    c_pad = jnp.concatenate([c_prompt, c_sample, jnp.zeros((rows - nb_p - nb_s, D_MODEL), F32)], axis=0)
    layers, mod_all = [], []
    for l in range(depth):
        w_in_x, w_vat, wq, wqs, wk, wv = _prep_layer(w_in[l], w_uq[l], w_ukv[l])
        layers.append(dict(
            g_norm=g_norm[l].reshape(1, D_MODEL), w_in_x=w_in_x, w_vat=w_vat, g_q=g_q[l].reshape(1, Q_LORA), wq=wq, wqs=wqs,
            g_kv=g_kv[l].reshape(1, KV_LORA), wk=wk, wv=wv, sink=sink[l],
            w_oa=w_oa[l].astype(BF16), w_ob=w_ob[l].astype(BF16), w_out=w_out[l].astype(BF16)))
        mod_all.append(_mod_call(c_pad, w_ada[l], b_ada[l].reshape(1, 3 * D_MODEL)))
    gf = g_final.reshape(1, D_MODEL)
    y_prompt = _trunk(x_prompt, layers, gf, mod_all, 0)
    y_sample = _trunk(x_sample, layers, gf, mod_all, nb_p)
    return (y_prompt, y_sample)
```

```python
import functools
import math

import jax
import jax.numpy as jnp
from jax import lax
from jax.experimental import pallas as pl
from jax.experimental.pallas import tpu as pltpu

D_MODEL = 1024
HEAD_DIM = 64
A_HEADS = 8
A_KV_HEADS = 2
A_GROUP = A_HEADS // A_KV_HEADS
A_WIDTH = A_HEADS * HEAD_DIM
A_KV_WIDTH = A_KV_HEADS * HEAD_DIM
WINDOW = 128
BLOCK = 128
B_HEADS = 8
Q_LORA = 384
KV_LORA = 256
NOPE_DIM = 64
ROPE_DIM = 32
V_DIM = 64
B_WIDTH = B_HEADS * V_DIM
ROPE_THETA = 10000.0
EPS = 1e-6
NEG_BIG = -1e30

LANES = 128
VMEM_LIMIT_BYTES = 48 * 1024 * 1024
VMEM_LIMIT_INPROJ_BYTES = 56 * 1024 * 1024
BF16_SUBLANES = 16
VT_ROWS = 64 + BF16_SUBLANES

C_QA = 0
C_KA = C_QA + A_HEADS * LANES
C_ZA = C_KA + A_KV_WIDTH
C_CQ = C_ZA + A_WIDTH
C_CKV = C_CQ + Q_LORA
C_KR = C_CKV + KV_LORA
C_KRS = C_KR + LANES
C_ZB = C_KRS + LANES
C_GM = C_ZB + B_WIDTH
C_END = C_GM + 2 * D_MODEL

BF16 = jnp.bfloat16
F32 = jnp.float32


def _silu(x):
    return x * (1.0 / (1.0 + jnp.exp(-x)))


def _sigmoid(x):
    return 1.0 / (1.0 + jnp.exp(-x))


def _dot(a, b):
    return jnp.dot(a, b, preferred_element_type=F32)


def _dot_nt(a, b):
    return lax.dot_general(a, b, (((1,), (1,)), ((), ())), preferred_element_type=F32)


def _mod_kernel(c_ref, w_ref, b_ref, o_ref):
    c = c_ref[...]
    o_ref[...] = jnp.dot(_silu(c), w_ref[...], preferred_element_type=F32,
                         precision=lax.Precision.HIGHEST) + b_ref[...]


def _mod_call(c_pad, w_ada, b_ada):
    rows = c_pad.shape[0]
    n = w_ada.shape[1]
    bn = 512
    return pl.pallas_call(
        _mod_kernel,
        out_shape=jax.ShapeDtypeStruct((rows, n), F32),
        grid=(n // bn,),
        in_specs=[
            pl.BlockSpec((rows, D_MODEL), lambda j: (0, 0)),
            pl.BlockSpec((D_MODEL, bn), lambda j: (0, j)),
            pl.BlockSpec((1, bn), lambda j: (0, j)),
        ],
        out_specs=pl.BlockSpec((rows, bn), lambda j: (0, j)),
        compiler_params=pltpu.CompilerParams(dimension_semantics=("arbitrary",),
                                             vmem_limit_bytes=VMEM_LIMIT_BYTES),
        name="mod",
    )(c_pad, w_ada, b_ada)


def _rms(x, gain):
    return x * lax.rsqrt(jnp.mean(x * x, axis=-1, keepdims=True) + EPS) * gain


def _inproj_kernel(x_ref, mod_ref, gn_ref, win_ref, wvat_ref, gq_ref, wq_ref, wqs_ref, gkv_ref, wk_ref, wv_ref,
                   qcos_ref, qsin_ref, kcos_ref, ksin_ref,
                   qa_ref, ka_ref, va_ref, za_ref, qm_ref, km_ref, vm_ref, zb_ref, gm_ref):
    x = x_ref[0]
    mod = mod_ref[0]
    shift = mod[:, :D_MODEL]
    scale = mod[:, D_MODEL:2 * D_MODEL]
    h = (_rms(x, gn_ref[...]) * (1.0 + scale) + shift).astype(BF16)

    def seg(lo, width):
        return _dot(h, win_ref[:, lo:lo + width])

    qa_ref[0] = seg(C_QA, A_HEADS * LANES).astype(BF16)
    ka_ref[0] = seg(C_KA, A_KV_WIDTH).astype(BF16)
    va_ref[0] = _dot_nt(wvat_ref[...], h).astype(BF16)
    za_ref[0] = _silu(seg(C_ZA, A_WIDTH))
    zb_ref[0] = _silu(seg(C_ZB, B_WIDTH))
    gm_ref[0] = _sigmoid(seg(C_GM, 2 * D_MODEL))

    lat = seg(C_CQ, C_ZB - C_CQ)
    cq = lat[:, :Q_LORA]
    ckv = lat[:, C_CKV - C_CQ:C_KR - C_CQ]
    kr_a = lat[:, C_KR - C_CQ:C_KRS - C_CQ]
    kr_b = lat[:, C_KRS - C_CQ:]

    cqn = _rms(cq, gq_ref[...]).astype(BF16)
    qcos = jnp.concatenate([qcos_ref[...]] * 2, axis=1)
    qsin = jnp.concatenate([qsin_ref[...]] * 2, axis=1)
    for pair in range(B_HEADS // 2):
        sl = slice(pair * 2 * LANES, (pair + 1) * 2 * LANES)
        qh = _dot(cqn, wq_ref[:, sl]) * qcos + _dot(cqn, wqs_ref[:, sl]) * qsin
        qm_ref[0, :, sl] = qh.astype(BF16)

    ckvn = _rms(ckv, gkv_ref[...]).astype(BF16)
    kr = kr_a * kcos_ref[...] + kr_b * ksin_ref[...]
    kr = jnp.concatenate([kr, kr], axis=1)
    for pair in range(B_HEADS // 2):
        sl = slice(pair * 2 * LANES, (pair + 1) * 2 * LANES)
        km_ref[0, :, sl] = (_dot(ckvn, wk_ref[:, sl]) + kr).astype(BF16)
    bk = vm_ref.shape[-1]
    row = lax.broadcasted_iota(jnp.int32, (B_HEADS * VT_ROWS, bk), 0)
    is_one = row % VT_ROWS >= V_DIM
    for j in range(vm_ref.shape[1]):
        vt = _dot_nt(wv_ref[...], ckvn[j * bk:(j + 1) * bk])
        vm_ref[0, j] = jnp.where(is_one, 1.0, vt).astype(BF16)


def _inproj_call(x, mod3, g_norm, w_in_x, w_vat, g_q, wq, wqs, g_kv, wk, wv, tabs, tm, bk):
    B, S, _ = x.shape
    nt = S // tm
    nsub = tm // bk
    const2 = lambda b, i: (0, 0)
    tok = lambda w: pl.BlockSpec((1, tm, w), lambda b, i: (b, i, 0))
    tab = pl.BlockSpec((tm, LANES), lambda b, i: (i, 0))
    full = lambda a: pl.BlockSpec(a.shape, const2)
    outs = [((B, S, A_HEADS * LANES), BF16, tok(A_HEADS * LANES)),
            ((B, S, A_KV_WIDTH), BF16, tok(A_KV_WIDTH)),
            ((B, A_KV_WIDTH, S), BF16, pl.BlockSpec((1, A_KV_WIDTH, tm), lambda b, i: (b, 0, i))),
            ((B, S, A_WIDTH), F32, tok(A_WIDTH)),
            ((B, S, B_HEADS * LANES), BF16, tok(B_HEADS * LANES)),
            ((B, S, B_HEADS * LANES), BF16, tok(B_HEADS * LANES)),
            ((B, nt * nsub, B_HEADS * VT_ROWS, bk), BF16,
             pl.BlockSpec((1, nsub, B_HEADS * VT_ROWS, bk), lambda b, i: (b, i, 0, 0))),
            ((B, S, B_WIDTH), F32, tok(B_WIDTH)),
            ((B, S, 2 * D_MODEL), F32, tok(2 * D_MODEL))]
    out_shape = [jax.ShapeDtypeStruct(shape, dt) for shape, dt, _ in outs]
    out_specs = [spec for _, _, spec in outs]
    return pl.pallas_call(
        _inproj_kernel,
        out_shape=out_shape,
        grid=(B, nt),
        in_specs=[tok(D_MODEL),
                  pl.BlockSpec((1, 1, 3 * D_MODEL), lambda b, i: (b, 0, 0)),
                  full(g_norm), full(w_in_x), full(w_vat), full(g_q), full(wq), full(wqs), full(g_kv), full(wk),
                  full(wv), tab, tab, tab, tab],
        out_specs=out_specs,
        compiler_params=pltpu.CompilerParams(dimension_semantics=("arbitrary", "arbitrary"),
                                             vmem_limit_bytes=VMEM_LIMIT_INPROJ_BYTES),
        name="inproj",
    )(x, mod3, g_norm, w_in_x, w_vat, g_q, wq, wqs, g_kv, wk, wv, *tabs)


def _swa_kernel(sink_ref, q_ref, kp_ref, kc_ref, kn_ref, vp_ref, vc_ref, vn_ref, za_ref, o_ref, bias_ref,
                *, nsub, slopes):
    i = pl.program_id(1)
    last = pl.num_programs(1) - 1

    @pl.when(i == 0)
    def _():
        row = lax.broadcasted_iota(jnp.int32, (BLOCK, BLOCK), 0)
        col = lax.broadcasted_iota(jnp.int32, (BLOCK, BLOCK), 1)
        for t in range(3):
            rel = row - col + (t - 1) * BLOCK
            dist = jnp.abs(rel).astype(F32)
            valid = jnp.abs(rel) <= WINDOW
            for head in range(A_HEADS):
                bias_ref[head, t] = jnp.where(valid, -slopes[head] * dist, NEG_BIG)

    edge_lo = jnp.where(i > 0, 0.0, NEG_BIG).astype(F32)
    edge_hi = jnp.where(i < last, 0.0, NEG_BIG).astype(F32)
    lane2 = lax.broadcasted_iota(jnp.int32, (1, 2 * BLOCK), 1)

    def k_block(jj):
        if jj < 0:
            return kp_ref[0]
        if jj >= nsub:
            return kn_ref[0]
        return kc_ref[0, jj * BLOCK:(jj + 1) * BLOCK, :]

    def vt_block(jj, vrows):
        if jj < 0:
            return vp_ref[0, vrows, :]
        if jj >= nsub:
            return vn_ref[0, vrows, :]
        return vc_ref[0, vrows, jj * BLOCK:(jj + 1) * BLOCK]

    def scores(unit):
        j, pair = unit
        rows = slice(j * BLOCK, (j + 1) * BLOCK)
        q2 = jnp.concatenate([q_ref[0, rows, (2 * pair + e) * LANES:(2 * pair + e + 1) * LANES] for e in range(2)],
                             axis=0)
        out = []
        for t in range(3):
            s = _dot_nt(k_block(j - 1 + t), q2)
            s = s + jnp.concatenate([bias_ref[2 * pair, t], bias_ref[2 * pair + 1, t]], axis=1)
            if j == 0 and t == 0:
                s = s + edge_lo
            if j == nsub - 1 and t == 2:
                s = s + edge_hi
            out.append(s)
        return out

    def finish(unit, s_blocks):
        j, pair = unit
        rows = slice(j * BLOCK, (j + 1) * BLOCK)
        kvh = (2 * pair) // A_GROUP
        vrows = slice(kvh * HEAD_DIM, (kvh + 1) * HEAD_DIM)
        sink = jnp.where(lane2 < BLOCK, sink_ref[2 * pair], sink_ref[2 * pair + 1])
        m = jnp.maximum(jnp.maximum(s_blocks[0], s_blocks[1]), s_blocks[2])
        m = jnp.maximum(jnp.max(m, axis=0, keepdims=True), sink)
        denom = jnp.exp(sink - m)
        ps = []
        for t in range(3):
            p = jnp.exp(s_blocks[t] - m)
            denom = denom + jnp.sum(p, axis=0, keepdims=True)
            ps.append(p.astype(BF16))
        vt = jnp.concatenate([vt_block(j - 1 + t, vrows) for t in range(3)], axis=1)
        ot = _dot(vt, jnp.concatenate(ps, axis=0)) * (1.0 / denom)
        o2 = jnp.concatenate([ot[:, :BLOCK], ot[:, BLOCK:]], axis=0)
        gate = za_ref[0, rows, pair * LANES:(pair + 1) * LANES]
        o_ref[0, rows, pair * LANES:(pair + 1) * LANES] = (o2.T * gate).astype(BF16)

    units = [(j, pair) for j in range(nsub) for pair in range(A_HEADS // 2)]
    pending = scores(units[0])
    for idx, unit in enumerate(units):
        ahead = scores(units[idx + 1]) if idx + 1 < len(units) else None
        finish(unit, pending)
        pending = ahead


def _swa_call(sink, qa, ka, vat, za, tq):
    B, S, _ = qa.shape
    nsub = tq // BLOCK
    nt = S // tq
    nblk = S // BLOCK
    slopes = tuple(float((2.0 ** (-8.0 / A_HEADS)) ** (k + 1)) for k in range(A_HEADS))
    cur = lambda w: pl.BlockSpec((1, tq, w), lambda b, i: (b, i, 0))
    prev_blk = lambda i: jnp.maximum(i * nsub - 1, 0)
    next_blk = lambda i: jnp.minimum((i + 1) * nsub, nblk - 1)
    k_prev = pl.BlockSpec((1, BLOCK, LANES), lambda b, i: (b, prev_blk(i), 0))
    k_next = pl.BlockSpec((1, BLOCK, LANES), lambda b, i: (b, next_blk(i), 0))
    v_prev = pl.BlockSpec((1, A_KV_WIDTH, BLOCK), lambda b, i: (b, 0, prev_blk(i)))
    v_cur = pl.BlockSpec((1, A_KV_WIDTH, tq), lambda b, i: (b, 0, i))
    v_next = pl.BlockSpec((1, A_KV_WIDTH, BLOCK), lambda b, i: (b, 0, next_blk(i)))
    return pl.pallas_call(
        functools.partial(_swa_kernel, nsub=nsub, slopes=slopes),
        out_shape=jax.ShapeDtypeStruct((B, S, A_WIDTH), BF16),
        grid=(B, nt),
        in_specs=[pl.BlockSpec(memory_space=pltpu.SMEM),
                  cur(A_HEADS * LANES), k_prev, cur(LANES), k_next, v_prev, v_cur, v_next, cur(A_WIDTH)],
        out_specs=cur(A_WIDTH),
        scratch_shapes=[pltpu.VMEM((A_HEADS, 3, BLOCK, BLOCK), F32)],
        compiler_params=pltpu.CompilerParams(dimension_semantics=("arbitrary", "arbitrary"),
                                             vmem_limit_bytes=VMEM_LIMIT_BYTES),
        name="swa",
    )(sink, qa, ka, ka, ka, vat, vat, vat, za)


def _mla_kernel(q_ref, k_ref, v_ref, zb_ref, o_ref, qt_ref, s_ref, mu_ref, m_ref, acc_ref, *, nk, unroll):
    m_ref[...] = jnp.full(m_ref.shape, NEG_BIG, F32)
    acc_ref[...] = jnp.zeros(acc_ref.shape, F32)

    for e in range(2):
        qt_ref[e] = q_ref[0, :, e * LANES:(e + 1) * LANES].astype(F32).T.astype(BF16)

    def scores(c, slot, e):
        lanes = slice(e * LANES, (e + 1) * LANES)
        s = _dot(k_ref[0, c, :, lanes], qt_ref[e])
        s_ref[slot, e] = s
        mu_ref[slot, e] = jnp.max(s, axis=0, keepdims=True)

    def update(c, slot, e):
        m_old = m_ref[e]
        m_new = jnp.maximum(m_old, mu_ref[slot, e])
        alpha = jnp.exp2(m_old - m_new)
        p = jnp.exp2(s_ref[slot, e] - m_new)
        vt = v_ref[0, c, e * VT_ROWS:(e + 1) * VT_ROWS, :]
        acc_ref[e] = alpha * acc_ref[e] + _dot(vt, p.astype(BF16))
        m_ref[e] = m_new

    def step(c_next, c, slot):
        for e in range(2):
            scores(c_next, 1 - slot, e)
            update(c, slot, e)

    for e in range(2):
        scores(0, 0, e)

    def body(j, carry):
        c = unroll * j
        for u in range(unroll):
            step(c + u + 1, c + u, u % 2)
        return carry

    lax.fori_loop(0, nk // unroll - 1, body, 0)
    for u in range(unroll - 1):
        step(nk - unroll + u + 1, nk - unroll + u, u % 2)
    for e in range(2):
        update(nk - 1, (unroll - 1) % 2, e)

    ot = jnp.concatenate([acc_ref[e, :V_DIM] * (1.0 / acc_ref[e, V_DIM:V_DIM + 1]) for e in range(2)],
                         axis=0)
    o_ref[0] = (ot.T * zb_ref[0]).astype(BF16)


def _mla_call(qm, km, vm, zb, bq):
    B, S, _ = qm.shape
    _, nk, _, bk = vm.shape
    npair = B_HEADS // 2
    nq = S // bq
    km = km.reshape(B, nk, bk, B_HEADS * LANES)
    unroll = 16
    assert nk % unroll == 0
    return pl.pallas_call(
        functools.partial(_mla_kernel, nk=nk, unroll=unroll),
        out_shape=jax.ShapeDtypeStruct((B, S, B_WIDTH), BF16),
        grid=(B, npair, nq),
        in_specs=[pl.BlockSpec((1, bq, 2 * LANES), lambda b, p, i: (b, i, p)),
                  pl.BlockSpec((1, nk, bk, 2 * LANES), lambda b, p, i: (b, 0, 0, p)),
                  pl.BlockSpec((1, nk, 2 * VT_ROWS, bk), lambda b, p, i: (b, 0, p, 0)),
                  pl.BlockSpec((1, bq, LANES), lambda b, p, i: (b, i, p))],
        out_specs=pl.BlockSpec((1, bq, LANES), lambda b, p, i: (b, i, p)),
        scratch_shapes=[pltpu.VMEM((2, LANES, bq), BF16),
                        pltpu.VMEM((2, 2, bk, bq), F32), pltpu.VMEM((2, 2, 1, bq), F32),
                        pltpu.VMEM((2, 1, bq), F32), pltpu.VMEM((2, VT_ROWS, bq), F32)],
        compiler_params=pltpu.CompilerParams(dimension_semantics=("arbitrary", "arbitrary", "arbitrary"),
                                             vmem_limit_bytes=VMEM_LIMIT_BYTES),
        name="mla",
    )(qm, km, vm, zb)


def _outproj_kernel(x_ref, mod_ref, ya_ref, yb_ref, gm_ref, woa_ref, wob_ref, wout_ref, gf_ref, o_ref, *, final):
    ga = gm_ref[0, :, :D_MODEL]
    gb = gm_ref[0, :, D_MODEL:]
    merged = ga * _dot(ya_ref[0], woa_ref[...]) + gb * _dot(yb_ref[0], wob_ref[...])
    gate_res = mod_ref[0][:, 2 * D_MODEL:]
    y = x_ref[0] + gate_res * _dot(merged.astype(BF16), wout_ref[...])
    if final:
        y = _rms(y, gf_ref[...])
    o_ref[0] = y


def _outproj_call(x, mod3, ya, yb, gm, w_oa, w_ob, w_out, g_final, tm, final):
    B, S, _ = x.shape
    nt = S // tm
    const2 = lambda b, i: (0, 0)
    tok = lambda w: pl.BlockSpec((1, tm, w), lambda b, i: (b, i, 0))
    full = lambda a: pl.BlockSpec(a.shape, const2)
    return pl.pallas_call(
        functools.partial(_outproj_kernel, final=final),
        out_shape=jax.ShapeDtypeStruct((B, S, D_MODEL), F32),
        grid=(B, nt),
        in_specs=[tok(D_MODEL),
                  pl.BlockSpec((1, 1, 3 * D_MODEL), lambda b, i: (b, 0, 0)),
                  tok(A_WIDTH), tok(B_WIDTH), tok(2 * D_MODEL),
                  full(w_oa), full(w_ob), full(w_out), full(g_final)],
        out_specs=tok(D_MODEL),
        compiler_params=pltpu.CompilerParams(dimension_semantics=("arbitrary", "arbitrary"),
                                             vmem_limit_bytes=VMEM_LIMIT_BYTES),
        name="outproj",
    )(x, mod3, ya, yb, gm, w_oa, w_ob, w_out, g_final)


def _head_groups(w):
    z = jnp.zeros((w.shape[0], HEAD_DIM), w.dtype)
    groups = []
    for head in range(A_HEADS):
        wh = w[:, head * HEAD_DIM:(head + 1) * HEAD_DIM]
        groups += [wh, z] if head // A_GROUP == 0 else [z, wh]
    return jnp.concatenate(groups, axis=1)


def _swap_halves(w):
    half = w.shape[-1] // 2
    return jnp.concatenate([w[..., half:], w[..., :half]], axis=-1)


def _place_rope(w):
    d = w.shape[0]
    return jnp.concatenate([jnp.zeros((d, NOPE_DIM), w.dtype), w, jnp.zeros((d, LANES - NOPE_DIM - ROPE_DIM), w.dtype)], axis=1)


def _prep_layer(w_in, w_uq, w_ukv):
    o = 0
    parts = {}
    for name, width in (("qa", A_WIDTH), ("ka", A_KV_WIDTH), ("va", A_KV_WIDTH), ("za", A_WIDTH), ("cq", Q_LORA),
                        ("ckv", KV_LORA), ("kr", ROPE_DIM), ("zb", B_WIDTH), ("gm", 2 * D_MODEL)):
        parts[name] = w_in[:, o:o + width]
        o += width
    w_in_x = jnp.concatenate([
        _head_groups(parts["qa"] * (HEAD_DIM ** -0.5)), parts["ka"], parts["za"],
        parts["cq"], parts["ckv"], _place_rope(parts["kr"]), _place_rope(_swap_halves(parts["kr"])),
        parts["zb"], parts["gm"]], axis=1).astype(BF16)
    assert w_in_x.shape[1] == C_END
    w_vat = parts["va"].T.astype(BF16)

    uq = w_uq.reshape(Q_LORA, B_HEADS, NOPE_DIM + ROPE_DIM)
    zq = jnp.zeros((Q_LORA, B_HEADS, LANES - NOPE_DIM - ROPE_DIM), w_uq.dtype)
    wq = jnp.concatenate([uq, zq], axis=-1).reshape(Q_LORA, B_HEADS * LANES).astype(BF16)
    wqs = jnp.concatenate([jnp.zeros((Q_LORA, B_HEADS, NOPE_DIM), w_uq.dtype), _swap_halves(uq[..., NOPE_DIM:]), zq],
                          axis=-1).reshape(Q_LORA, B_HEADS * LANES).astype(BF16)
    ukv = w_ukv.reshape(KV_LORA, B_HEADS, NOPE_DIM + V_DIM)
    wk = jnp.concatenate([ukv[..., :NOPE_DIM], jnp.zeros((KV_LORA, B_HEADS, LANES - NOPE_DIM), w_ukv.dtype)],
                         axis=-1).reshape(KV_LORA, B_HEADS * LANES).astype(BF16)
    wv = jnp.transpose(ukv[..., NOPE_DIM:], (1, 2, 0))
    wv = jnp.concatenate([wv, jnp.zeros((B_HEADS, VT_ROWS - V_DIM, KV_LORA), w_ukv.dtype)], axis=1)
    wv = wv.reshape(B_HEADS * VT_ROWS, KV_LORA).astype(BF16)
    return w_in_x, w_vat, wq, wqs, wk, wv


def _rope_tables(S):
    inv = 1.0 / (ROPE_THETA ** (jnp.arange(0, ROPE_DIM, 2, dtype=F32) / ROPE_DIM))
    ang = jnp.arange(S, dtype=F32)[:, None] * inv[None, :]
    cos, sin = jnp.cos(ang), jnp.sin(ang)
    ones = jnp.ones((S, NOPE_DIM), F32)
    zeros_n = jnp.zeros((S, NOPE_DIM), F32)
    zeros_p = jnp.zeros((S, LANES - NOPE_DIM - ROPE_DIM), F32)
    cos_t = jnp.concatenate([ones, cos, cos, zeros_p], axis=1)
    sin_t = jnp.concatenate([zeros_n, -sin, sin, zeros_p], axis=1)
    qscale = (NOPE_DIM + ROPE_DIM) ** -0.5 * math.log2(math.e)
    return cos_t * qscale, sin_t * qscale, cos_t, sin_t


TILE_PROJ = 512
TILE_SWA = 512
TILE_MLA_Q = 512
TILE_MLA_K = 256


def _trunk(x, layers, g_final, mod_all, row0):
    B, S, _ = x.shape
    tabs = _rope_tables(S)
    n_layers = len(layers)
    for l, lw in enumerate(layers):
        mod3 = mod_all[l][row0:row0 + B].reshape(B, 1, 3 * D_MODEL)
        qa, ka, va, za, qm, km, vm, zb, gm = _inproj_call(
            x, mod3, lw["g_norm"], lw["w_in_x"], lw["w_vat"], lw["g_q"], lw["wq"], lw["wqs"], lw["g_kv"], lw["wk"],
            lw["wv"],
            tabs, TILE_PROJ, TILE_MLA_K)
        ya = _swa_call(lw["sink"], qa, ka, va, za, TILE_SWA)
        yb = _mla_call(qm, km, vm, zb, TILE_MLA_Q)
        x = _outproj_call(x, mod3, ya, yb, gm, lw["w_oa"], lw["w_ob"], lw["w_out"], g_final, TILE_PROJ,
                          final=(l == n_layers - 1))
    return x


def kernel(x_prompt, x_sample, c_prompt, c_sample, w_ada, b_ada, g_norm, w_in, g_q, w_uq, g_kv, w_ukv, sink,
           w_oa, w_ob, w_out, g_final):
    depth = w_in.shape[0]
    nb_p, nb_s = c_prompt.shape[0], c_sample.shape[0]
    rows = -(-(nb_p + nb_s) // 8) * 8
    c_pad = jnp.concatenate([c_prompt, c_sample, jnp.zeros((rows - nb_p - nb_s, D_MODEL), F32)], axis=0)
    layers, mod_all = [], []
    for l in range(depth):
        w_in_x, w_vat, wq, wqs, wk, wv = _prep_layer(w_in[l], w_uq[l], w_ukv[l])
        layers.append(dict(
            g_norm=g_norm[l].reshape(1, D_MODEL), w_in_x=w_in_x, w_vat=w_vat, g_q=g_q[l].reshape(1, Q_LORA), wq=wq, wqs=wqs,
            g_kv=g_kv[l].reshape(1, KV_LORA), wk=wk, wv=wv, sink=sink[l],
            w_oa=w_oa[l].astype(BF16), w_ob=w_ob[l].astype(BF16), w_out=w_out[l].astype(BF16)))
        mod_all.append(_mod_call(c_pad, w_ada[l], b_ada[l].reshape(1, 3 * D_MODEL)))
    gf = g_final.reshape(1, D_MODEL)
    y_prompt = _trunk(x_prompt, layers, gf, mod_all, 0)
    y_sample = _trunk(x_sample, layers, gf, mod_all, nb_p)
    return (y_prompt, y_sample)
```

```python
import functools
import math

import jax
import jax.numpy as jnp
from jax import lax
from jax.experimental import pallas as pl
from jax.experimental.pallas import tpu as pltpu

D_MODEL = 1024
HEAD_DIM = 64
A_HEADS = 8
A_KV_HEADS = 2
A_GROUP = A_HEADS // A_KV_HEADS
A_WIDTH = A_HEADS * HEAD_DIM
A_KV_WIDTH = A_KV_HEADS * HEAD_DIM
WINDOW = 128
BLOCK = 128
B_HEADS = 8
Q_LORA = 384
KV_LORA = 256
NOPE_DIM = 64
ROPE_DIM = 32
V_DIM = 64
B_WIDTH = B_HEADS * V_DIM
ROPE_THETA = 10000.0
EPS = 1e-6
NEG_BIG = -1e30

LANES = 128
VMEM_LIMIT_BYTES = 48 * 1024 * 1024
VMEM_LIMIT_INPROJ_BYTES = 56 * 1024 * 1024
BF16_SUBLANES = 16
VT_ROWS = 64 + BF16_SUBLANES

C_QA = 0
C_KA = C_QA + A_HEADS * LANES
C_ZA = C_KA + A_KV_WIDTH
C_CQ = C_ZA + A_WIDTH
C_CKV = C_CQ + Q_LORA
C_KR = C_CKV + KV_LORA
C_KRS = C_KR + LANES
C_ZB = C_KRS + LANES
C_GM = C_ZB + B_WIDTH
C_END = C_GM + 2 * D_MODEL

BF16 = jnp.bfloat16
F32 = jnp.float32


def _silu(x):
    return x * (1.0 / (1.0 + jnp.exp(-x)))


def _sigmoid(x):
    return 1.0 / (1.0 + jnp.exp(-x))


def _dot(a, b):
    return jnp.dot(a, b, preferred_element_type=F32)


def _dot_nt(a, b):
    return lax.dot_general(a, b, (((1,), (1,)), ((), ())), preferred_element_type=F32)


def _mod_kernel(c_ref, w_ref, b_ref, o_ref):
    c = c_ref[...]
    o_ref[...] = jnp.dot(_silu(c), w_ref[...], preferred_element_type=F32,
                         precision=lax.Precision.HIGHEST) + b_ref[...]


def _mod_call(c_pad, w_ada, b_ada):
    rows = c_pad.shape[0]
    n = w_ada.shape[1]
    bn = 512
    return pl.pallas_call(
        _mod_kernel,
        out_shape=jax.ShapeDtypeStruct((rows, n), F32),
        grid=(n // bn,),
        in_specs=[
            pl.BlockSpec((rows, D_MODEL), lambda j: (0, 0)),
            pl.BlockSpec((D_MODEL, bn), lambda j: (0, j)),
            pl.BlockSpec((1, bn), lambda j: (0, j)),
        ],
        out_specs=pl.BlockSpec((rows, bn), lambda j: (0, j)),
        compiler_params=pltpu.CompilerParams(dimension_semantics=("arbitrary",),
                                             vmem_limit_bytes=VMEM_LIMIT_BYTES),
        name="mod",
    )(c_pad, w_ada, b_ada)


def _rms(x, gain):
    return x * lax.rsqrt(jnp.mean(x * x, axis=-1, keepdims=True) + EPS) * gain


def _inproj_kernel(x_ref, mod_ref, gn_ref, win_ref, wvat_ref, gq_ref, wq_ref, wqs_ref, gkv_ref, wk_ref, wv_ref,
                   qcos_ref, qsin_ref, kcos_ref, ksin_ref,
                   qa_ref, ka_ref, va_ref, za_ref, qm_ref, km_ref, vm_ref, zb_ref, gm_ref):
    x = x_ref[0]
    mod = mod_ref[0]
    shift = mod[:, :D_MODEL]
    scale = mod[:, D_MODEL:2 * D_MODEL]
    h = (_rms(x, gn_ref[...]) * (1.0 + scale) + shift).astype(BF16)

    def seg(lo, width):
        return _dot(h, win_ref[:, lo:lo + width])

    qa_ref[0] = seg(C_QA, A_HEADS * LANES).astype(BF16)
    ka_ref[0] = seg(C_KA, A_KV_WIDTH).astype(BF16)
    va_ref[0] = _dot_nt(wvat_ref[...], h).astype(BF16)
    za_ref[0] = _silu(seg(C_ZA, A_WIDTH))
    zb_ref[0] = _silu(seg(C_ZB, B_WIDTH))
    gm_ref[0] = _sigmoid(seg(C_GM, 2 * D_MODEL))

    lat = seg(C_CQ, C_ZB - C_CQ)
    cq = lat[:, :Q_LORA]
    ckv = lat[:, C_CKV - C_CQ:C_KR - C_CQ]
    kr_a = lat[:, C_KR - C_CQ:C_KRS - C_CQ]
    kr_b = lat[:, C_KRS - C_CQ:]

    cqn = _rms(cq, gq_ref[...]).astype(BF16)
    qcos = jnp.concatenate([qcos_ref[...]] * 2, axis=1)
    qsin = jnp.concatenate([qsin_ref[...]] * 2, axis=1)
    for pair in range(B_HEADS // 2):
        sl = slice(pair * 2 * LANES, (pair + 1) * 2 * LANES)
        qh = _dot(cqn, wq_ref[:, sl]) * qcos + _dot(cqn, wqs_ref[:, sl]) * qsin
        qm_ref[0, sl, :] = qh.T.astype(BF16)

    ckvn = _rms(ckv, gkv_ref[...]).astype(BF16)
    kr = kr_a * kcos_ref[...] + kr_b * ksin_ref[...]
    kr = jnp.concatenate([kr, kr], axis=1)
    for pair in range(B_HEADS // 2):
        sl = slice(pair * 2 * LANES, (pair + 1) * 2 * LANES)
        km_ref[0, :, sl] = (_dot(ckvn, wk_ref[:, sl]) + kr).astype(BF16)
    bk = vm_ref.shape[-1]
    row = lax.broadcasted_iota(jnp.int32, (B_HEADS * VT_ROWS, bk), 0)
    is_one = row % VT_ROWS >= V_DIM
    for j in range(vm_ref.shape[1]):
        vt = _dot_nt(wv_ref[...], ckvn[j * bk:(j + 1) * bk])
        vm_ref[0, j] = jnp.where(is_one, 1.0, vt).astype(BF16)


def _inproj_call(x, mod3, g_norm, w_in_x, w_vat, g_q, wq, wqs, g_kv, wk, wv, tabs, tm, bk):
    B, S, _ = x.shape
    nt = S // tm
    nsub = tm // bk
    const2 = lambda b, i: (0, 0)
    tok = lambda w: pl.BlockSpec((1, tm, w), lambda b, i: (b, i, 0))
    tab = pl.BlockSpec((tm, LANES), lambda b, i: (i, 0))
    full = lambda a: pl.BlockSpec(a.shape, const2)
    outs = [((B, S, A_HEADS * LANES), BF16, tok(A_HEADS * LANES)),
            ((B, S, A_KV_WIDTH), BF16, tok(A_KV_WIDTH)),
            ((B, A_KV_WIDTH, S), BF16, pl.BlockSpec((1, A_KV_WIDTH, tm), lambda b, i: (b, 0, i))),
            ((B, S, A_WIDTH), F32, tok(A_WIDTH)),
            ((B, B_HEADS * LANES, S), BF16, pl.BlockSpec((1, B_HEADS * LANES, tm), lambda b, i: (b, 0, i))),
            ((B, S, B_HEADS * LANES), BF16, tok(B_HEADS * LANES)),
            ((B, nt * nsub, B_HEADS * VT_ROWS, bk), BF16,
             pl.BlockSpec((1, nsub, B_HEADS * VT_ROWS, bk), lambda b, i: (b, i, 0, 0))),
            ((B, S, B_WIDTH), F32, tok(B_WIDTH)),
            ((B, S, 2 * D_MODEL), F32, tok(2 * D_MODEL))]
    out_shape = [jax.ShapeDtypeStruct(shape, dt) for shape, dt, _ in outs]
    out_specs = [spec for _, _, spec in outs]
    return pl.pallas_call(
        _inproj_kernel,
        out_shape=out_shape,
        grid=(B, nt),
        in_specs=[tok(D_MODEL),
                  pl.BlockSpec((1, 1, 3 * D_MODEL), lambda b, i: (b, 0, 0)),
                  full(g_norm), full(w_in_x), full(w_vat), full(g_q), full(wq), full(wqs), full(g_kv), full(wk),
                  full(wv), tab, tab, tab, tab],
        out_specs=out_specs,
        compiler_params=pltpu.CompilerParams(dimension_semantics=("arbitrary", "arbitrary"),
                                             vmem_limit_bytes=VMEM_LIMIT_INPROJ_BYTES),
        name="inproj",
    )(x, mod3, g_norm, w_in_x, w_vat, g_q, wq, wqs, g_kv, wk, wv, *tabs)


def _swa_kernel(sink_ref, q_ref, kp_ref, kc_ref, kn_ref, vp_ref, vc_ref, vn_ref, za_ref, o_ref, bias_ref,
                *, nsub, slopes):
    i = pl.program_id(1)
    last = pl.num_programs(1) - 1

    @pl.when(i == 0)
    def _():
        row = lax.broadcasted_iota(jnp.int32, (BLOCK, BLOCK), 0)
        col = lax.broadcasted_iota(jnp.int32, (BLOCK, BLOCK), 1)
        for t in range(3):
            rel = row - col + (t - 1) * BLOCK
            dist = jnp.abs(rel).astype(F32)
            valid = jnp.abs(rel) <= WINDOW
            for head in range(A_HEADS):
                bias_ref[head, t] = jnp.where(valid, -slopes[head] * dist, NEG_BIG)

    edge_lo = jnp.where(i > 0, 0.0, NEG_BIG).astype(F32)
    edge_hi = jnp.where(i < last, 0.0, NEG_BIG).astype(F32)
    lane2 = lax.broadcasted_iota(jnp.int32, (1, 2 * BLOCK), 1)

    def k_block(jj):
        if jj < 0:
            return kp_ref[0]
        if jj >= nsub:
            return kn_ref[0]
        return kc_ref[0, jj * BLOCK:(jj + 1) * BLOCK, :]

    def vt_block(jj, vrows):
        if jj < 0:
            return vp_ref[0, vrows, :]
        if jj >= nsub:
            return vn_ref[0, vrows, :]
        return vc_ref[0, vrows, jj * BLOCK:(jj + 1) * BLOCK]

    def scores(unit):
        j, pair = unit
        rows = slice(j * BLOCK, (j + 1) * BLOCK)
        q2 = jnp.concatenate([q_ref[0, rows, (2 * pair + e) * LANES:(2 * pair + e + 1) * LANES] for e in range(2)],
                             axis=0)
        out = []
        for t in range(3):
            s = _dot_nt(k_block(j - 1 + t), q2)
            s = s + jnp.concatenate([bias_ref[2 * pair, t], bias_ref[2 * pair + 1, t]], axis=1)
            if j == 0 and t == 0:
                s = s + edge_lo
            if j == nsub - 1 and t == 2:
                s = s + edge_hi
            out.append(s)
        return out

    def finish(unit, s_blocks):
        j, pair = unit
        rows = slice(j * BLOCK, (j + 1) * BLOCK)
        kvh = (2 * pair) // A_GROUP
        vrows = slice(kvh * HEAD_DIM, (kvh + 1) * HEAD_DIM)
        sink = jnp.where(lane2 < BLOCK, sink_ref[2 * pair], sink_ref[2 * pair + 1])
        m = jnp.maximum(jnp.maximum(s_blocks[0], s_blocks[1]), s_blocks[2])
        m = jnp.maximum(jnp.max(m, axis=0, keepdims=True), sink)
        denom = jnp.exp(sink - m)
        ps = []
        for t in range(3):
            p = jnp.exp(s_blocks[t] - m)
            denom = denom + jnp.sum(p, axis=0, keepdims=True)
            ps.append(p.astype(BF16))
        vt = jnp.concatenate([vt_block(j - 1 + t, vrows) for t in range(3)], axis=1)
        ot = _dot(vt, jnp.concatenate(ps, axis=0)) * (1.0 / denom)
        o2 = jnp.concatenate([ot[:, :BLOCK], ot[:, BLOCK:]], axis=0)
        gate = za_ref[0, rows, pair * LANES:(pair + 1) * LANES]
        o_ref[0, rows, pair * LANES:(pair + 1) * LANES] = (o2.T * gate).astype(BF16)

    units = [(j, pair) for j in range(nsub) for pair in range(A_HEADS // 2)]
    pending = scores(units[0])
    for idx, unit in enumerate(units):
        ahead = scores(units[idx + 1]) if idx + 1 < len(units) else None
        finish(unit, pending)
        pending = ahead


def _swa_call(sink, qa, ka, vat, za, tq):
    B, S, _ = qa.shape
    nsub = tq // BLOCK
    nt = S // tq
    nblk = S // BLOCK
    slopes = tuple(float((2.0 ** (-8.0 / A_HEADS)) ** (k + 1)) for k in range(A_HEADS))
    cur = lambda w: pl.BlockSpec((1, tq, w), lambda b, i: (b, i, 0))
    prev_blk = lambda i: jnp.maximum(i * nsub - 1, 0)
    next_blk = lambda i: jnp.minimum((i + 1) * nsub, nblk - 1)
    k_prev = pl.BlockSpec((1, BLOCK, LANES), lambda b, i: (b, prev_blk(i), 0))
    k_next = pl.BlockSpec((1, BLOCK, LANES), lambda b, i: (b, next_blk(i), 0))
    v_prev = pl.BlockSpec((1, A_KV_WIDTH, BLOCK), lambda b, i: (b, 0, prev_blk(i)))
    v_cur = pl.BlockSpec((1, A_KV_WIDTH, tq), lambda b, i: (b, 0, i))
    v_next = pl.BlockSpec((1, A_KV_WIDTH, BLOCK), lambda b, i: (b, 0, next_blk(i)))
    return pl.pallas_call(
        functools.partial(_swa_kernel, nsub=nsub, slopes=slopes),
        out_shape=jax.ShapeDtypeStruct((B, S, A_WIDTH), BF16),
        grid=(B, nt),
        in_specs=[pl.BlockSpec(memory_space=pltpu.SMEM),
                  cur(A_HEADS * LANES), k_prev, cur(LANES), k_next, v_prev, v_cur, v_next, cur(A_WIDTH)],
        out_specs=cur(A_WIDTH),
        scratch_shapes=[pltpu.VMEM((A_HEADS, 3, BLOCK, BLOCK), F32)],
        compiler_params=pltpu.CompilerParams(dimension_semantics=("arbitrary", "arbitrary"),
                                             vmem_limit_bytes=VMEM_LIMIT_BYTES),
        name="swa",
    )(sink, qa, ka, ka, ka, vat, vat, vat, za)


def _mla_kernel(q_ref, k_ref, v_ref, zb_ref, o_ref, s_ref, mu_ref, m_ref, acc_ref, *, nk, unroll):
    m_ref[...] = jnp.full(m_ref.shape, NEG_BIG, F32)
    acc_ref[...] = jnp.zeros(acc_ref.shape, F32)

    def scores(c, slot, e):
        lanes = slice(e * LANES, (e + 1) * LANES)
        s = _dot(k_ref[0, c, :, lanes], q_ref[0, lanes, :])
        s_ref[slot, e] = s
        mu_ref[slot, e] = jnp.max(s, axis=0, keepdims=True)

    def update(c, slot, e):
        m_old = m_ref[e]
        m_new = jnp.maximum(m_old, mu_ref[slot, e])
        alpha = jnp.exp2(m_old - m_new)
        p = jnp.exp2(s_ref[slot, e] - m_new)
        vt = v_ref[0, c, e * VT_ROWS:(e + 1) * VT_ROWS, :]
        acc_ref[e] = alpha * acc_ref[e] + _dot(vt, p.astype(BF16))
        m_ref[e] = m_new

    def step(c_next, c, slot):
        for e in range(2):
            scores(c_next, 1 - slot, e)
            update(c, slot, e)

    for e in range(2):
        scores(0, 0, e)

    def body(j, carry):
        c = unroll * j
        for u in range(unroll):
            step(c + u + 1, c + u, u % 2)
        return carry

    lax.fori_loop(0, nk // unroll - 1, body, 0)
    for u in range(unroll - 1):
        step(nk - unroll + u + 1, nk - unroll + u, u % 2)
    for e in range(2):
        update(nk - 1, (unroll - 1) % 2, e)

    ot = jnp.concatenate([acc_ref[e, :V_DIM] * (1.0 / acc_ref[e, V_DIM:V_DIM + 1]) for e in range(2)],
                         axis=0)
    o_ref[0] = (ot.T * zb_ref[0]).astype(BF16)


def _mla_call(qm, km, vm, zb, bq):
    B, _, S = qm.shape
    _, nk, _, bk = vm.shape
    npair = B_HEADS // 2
    nq = S // bq
    km = km.reshape(B, nk, bk, B_HEADS * LANES)
    unroll = 16
    assert nk % unroll == 0
    return pl.pallas_call(
        functools.partial(_mla_kernel, nk=nk, unroll=unroll),
        out_shape=jax.ShapeDtypeStruct((B, S, B_WIDTH), BF16),
        grid=(B, npair, nq),
        in_specs=[pl.BlockSpec((1, 2 * LANES, bq), lambda b, p, i: (b, p, i)),
                  pl.BlockSpec((1, nk, bk, 2 * LANES), lambda b, p, i: (b, 0, 0, p)),
                  pl.BlockSpec((1, nk, 2 * VT_ROWS, bk), lambda b, p, i: (b, 0, p, 0)),
                  pl.BlockSpec((1, bq, LANES), lambda b, p, i: (b, i, p))],
        out_specs=pl.BlockSpec((1, bq, LANES), lambda b, p, i: (b, i, p)),
        scratch_shapes=[pltpu.VMEM((2, 2, bk, bq), F32), pltpu.VMEM((2, 2, 1, bq), F32),
                        pltpu.VMEM((2, 1, bq), F32), pltpu.VMEM((2, VT_ROWS, bq), F32)],
        compiler_params=pltpu.CompilerParams(dimension_semantics=("arbitrary", "arbitrary", "arbitrary"),
                                             vmem_limit_bytes=VMEM_LIMIT_BYTES),
        name="mla",
    )(qm, km, vm, zb)


def _outproj_kernel(x_ref, mod_ref, ya_ref, yb_ref, gm_ref, woa_ref, wob_ref, wout_ref, gf_ref, o_ref, *, final):
    ga = gm_ref[0, :, :D_MODEL]
    gb = gm_ref[0, :, D_MODEL:]
    merged = ga * _dot(ya_ref[0], woa_ref[...]) + gb * _dot(yb_ref[0], wob_ref[...])
    gate_res = mod_ref[0][:, 2 * D_MODEL:]
    y = x_ref[0] + gate_res * _dot(merged.astype(BF16), wout_ref[...])
    if final:
        y = _rms(y, gf_ref[...])
    o_ref[0] = y


def _outproj_call(x, mod3, ya, yb, gm, w_oa, w_ob, w_out, g_final, tm, final):
    B, S, _ = x.shape
    nt = S // tm
    const2 = lambda b, i: (0, 0)
    tok = lambda w: pl.BlockSpec((1, tm, w), lambda b, i: (b, i, 0))
    full = lambda a: pl.BlockSpec(a.shape, const2)
    return pl.pallas_call(
        functools.partial(_outproj_kernel, final=final),
        out_shape=jax.ShapeDtypeStruct((B, S, D_MODEL), F32),
        grid=(B, nt),
        in_specs=[tok(D_MODEL),
                  pl.BlockSpec((1, 1, 3 * D_MODEL), lambda b, i: (b, 0, 0)),
                  tok(A_WIDTH), tok(B_WIDTH), tok(2 * D_MODEL),
                  full(w_oa), full(w_ob), full(w_out), full(g_final)],
        out_specs=tok(D_MODEL),
        compiler_params=pltpu.CompilerParams(dimension_semantics=("arbitrary", "arbitrary"),
                                             vmem_limit_bytes=VMEM_LIMIT_BYTES),
        name="outproj",
    )(x, mod3, ya, yb, gm, w_oa, w_ob, w_out, g_final)


def _head_groups(w):
    z = jnp.zeros((w.shape[0], HEAD_DIM), w.dtype)
    groups = []
    for head in range(A_HEADS):
        wh = w[:, head * HEAD_DIM:(head + 1) * HEAD_DIM]
        groups += [wh, z] if head // A_GROUP == 0 else [z, wh]
    return jnp.concatenate(groups, axis=1)


def _swap_halves(w):
    half = w.shape[-1] // 2
    return jnp.concatenate([w[..., half:], w[..., :half]], axis=-1)


def _place_rope(w):
    d = w.shape[0]
    return jnp.concatenate([jnp.zeros((d, NOPE_DIM), w.dtype), w, jnp.zeros((d, LANES - NOPE_DIM - ROPE_DIM), w.dtype)], axis=1)


def _prep_layer(w_in, w_uq, w_ukv):
    o = 0
    parts = {}
    for name, width in (("qa", A_WIDTH), ("ka", A_KV_WIDTH), ("va", A_KV_WIDTH), ("za", A_WIDTH), ("cq", Q_LORA),
                        ("ckv", KV_LORA), ("kr", ROPE_DIM), ("zb", B_WIDTH), ("gm", 2 * D_MODEL)):
        parts[name] = w_in[:, o:o + width]
        o += width
    w_in_x = jnp.concatenate([
        _head_groups(parts["qa"] * (HEAD_DIM ** -0.5)), parts["ka"], parts["za"],
        parts["cq"], parts["ckv"], _place_rope(parts["kr"]), _place_rope(_swap_halves(parts["kr"])),
        parts["zb"], parts["gm"]], axis=1).astype(BF16)
    assert w_in_x.shape[1] == C_END
    w_vat = parts["va"].T.astype(BF16)

    uq = w_uq.reshape(Q_LORA, B_HEADS, NOPE_DIM + ROPE_DIM)
    zq = jnp.zeros((Q_LORA, B_HEADS, LANES - NOPE_DIM - ROPE_DIM), w_uq.dtype)
    wq = jnp.concatenate([uq, zq], axis=-1).reshape(Q_LORA, B_HEADS * LANES).astype(BF16)
    wqs = jnp.concatenate([jnp.zeros((Q_LORA, B_HEADS, NOPE_DIM), w_uq.dtype), _swap_halves(uq[..., NOPE_DIM:]), zq],
                          axis=-1).reshape(Q_LORA, B_HEADS * LANES).astype(BF16)
    ukv = w_ukv.reshape(KV_LORA, B_HEADS, NOPE_DIM + V_DIM)
    wk = jnp.concatenate([ukv[..., :NOPE_DIM], jnp.zeros((KV_LORA, B_HEADS, LANES - NOPE_DIM), w_ukv.dtype)],
                         axis=-1).reshape(KV_LORA, B_HEADS * LANES).astype(BF16)
    wv = jnp.transpose(ukv[..., NOPE_DIM:], (1, 2, 0))
    wv = jnp.concatenate([wv, jnp.zeros((B_HEADS, VT_ROWS - V_DIM, KV_LORA), w_ukv.dtype)], axis=1)
    wv = wv.reshape(B_HEADS * VT_ROWS, KV_LORA).astype(BF16)
    return w_in_x, w_vat, wq, wqs, wk, wv


def _rope_tables(S):
    inv = 1.0 / (ROPE_THETA ** (jnp.arange(0, ROPE_DIM, 2, dtype=F32) / ROPE_DIM))
    ang = jnp.arange(S, dtype=F32)[:, None] * inv[None, :]
    cos, sin = jnp.cos(ang), jnp.sin(ang)
    ones = jnp.ones((S, NOPE_DIM), F32)
    zeros_n = jnp.zeros((S, NOPE_DIM), F32)
    zeros_p = jnp.zeros((S, LANES - NOPE_DIM - ROPE_DIM), F32)
    cos_t = jnp.concatenate([ones, cos, cos, zeros_p], axis=1)
    sin_t = jnp.concatenate([zeros_n, -sin, sin, zeros_p], axis=1)
    qscale = (NOPE_DIM + ROPE_DIM) ** -0.5 * math.log2(math.e)
    return cos_t * qscale, sin_t * qscale, cos_t, sin_t


TILE_PROJ = 512
TILE_SWA = 512
TILE_MLA_Q = 512
TILE_MLA_K = 256


def _trunk(x, layers, g_final, mod_all, row0):
    B, S, _ = x.shape
    tabs = _rope_tables(S)
    n_layers = len(layers)
    for l, lw in enumerate(layers):
        mod3 = mod_all[l][row0:row0 + B].reshape(B, 1, 3 * D_MODEL)
        qa, ka, va, za, qm, km, vm, zb, gm = _inproj_call(
            x, mod3, lw["g_norm"], lw["w_in_x"], lw["w_vat"], lw["g_q"], lw["wq"], lw["wqs"], lw["g_kv"], lw["wk"],
            lw["wv"],
            tabs, TILE_PROJ, TILE_MLA_K)
        ya = _swa_call(lw["sink"], qa, ka, va, za, TILE_SWA)
        yb = _mla_call(qm, km, vm, zb, TILE_MLA_Q)
        x = _outproj_call(x, mod3, ya, yb, gm, lw["w_oa"], lw["w_ob"], lw["w_out"], g_final, TILE_PROJ,
                          final=(l == n_layers - 1))
    return x


def kernel(x_prompt, x_sample, c_prompt, c_sample, w_ada, b_ada, g_norm, w_in, g_q, w_uq, g_kv, w_ukv, sink,
           w_oa, w_ob, w_out, g_final):
    depth = w_in.shape[0]
    nb_p, nb_s = c_prompt.shape[0], c_sample.shape[0]
    rows = -(-(nb_p + nb_s) // 8) * 8
    c_pad = jnp.concatenate([c_prompt, c_sample, jnp.zeros((rows - nb_p - nb_s, D_MODEL), F32)], axis=0)
    layers, mod_all = [], []
    for l in range(depth):
        w_in_x, w_vat, wq, wqs, wk, wv = _prep_layer(w_in[l], w_uq[l], w_ukv[l])
        layers.append(dict(
            g_norm=g_norm[l].reshape(1, D_MODEL), w_in_x=w_in_x, w_vat=w_vat, g_q=g_q[l].reshape(1, Q_LORA), wq=wq, wqs=wqs,
            g_kv=g_kv[l].reshape(1, KV_LORA), wk=wk, wv=wv, sink=sink[l],
            w_oa=w_oa[l].astype(BF16), w_ob=w_ob[l].astype(BF16), w_out=w_out[l].astype(BF16)))
        mod_all.append(_mod_call(c_pad, w_ada[l], b_ada[l].reshape(1, 3 * D_MODEL)))
    gf = g_final.reshape(1, D_MODEL)
    y_prompt = _trunk(x_prompt, layers, gf, mod_all, 0)
    y_sample = _trunk(x_sample, layers, gf, mod_all, nb_p)
    return (y_prompt, y_sample)
```
